```python
import math
import jax, jax.numpy as jnp
from jax import lax
import numpy as np

D_MODEL = 1024
BATCH = 4
SEQ = 4096
DEPTH = 4

GRID_W = 64
CTX_LEN = 256
EPS = 1e-6

SSD_INNER = D_MODEL
SSD_HEAD_DIM = 64
SSD_HEADS = SSD_INNER // SSD_HEAD_DIM
SSD_GROUPS = 2
SSD_STATE = 128
SSD_CONV = 5
SSD_CHUNK = 128
SSD_CONV_CH = SSD_INNER + 2 * SSD_GROUPS * SSD_STATE

HY_WIDTH = D_MODEL
HY_ORDER = 2
HY_SHORT = 3
HY_BANDS = 16
HY_EMB = 1 + 2 * HY_BANDS
HY_FF = 64
HY_TARGET = 1e-2
HY_FAST = 0.3
HY_SLOW = 1.5

DA_HEAD_DIM = 64
DA_HEADS = D_MODEL // (2 * DA_HEAD_DIM)
DA_WIDTH = DA_HEADS * 2 * DA_HEAD_DIM
Q_BLOCK = 128
ROPE_BASE = 10000.0

N_BRANCH = 3
BRANCH_W = D_MODEL

D_FF = ((8 * D_MODEL // 3 + 127) // 128) * 128
FFN_CONV = 3

IN_SSD = SSD_INNER + SSD_CONV_CH + 2 * SSD_HEADS
IN_HY = (HY_ORDER + 1) * HY_WIDTH
IN_DA = 3 * DA_WIDTH
IN_GATE = N_BRANCH * D_MODEL
OFF_HY = IN_SSD
OFF_DA = OFF_HY + IN_HY
OFF_GATE = OFF_DA + IN_DA
IN_COLS = OFF_GATE + IN_GATE

kernel_name = "hybrid_ssd_hyena_diffattn_dit_block"

F32 = jnp.float32


def rms_norm(x, g):
    xf = x.astype(F32)
    y = xf * lax.rsqrt(jnp.mean(xf * xf, axis=-1, keepdims=True) + EPS)
    return (y * g.astype(F32)).astype(x.dtype)


def dwconv(x, w, b):
    pad = w.shape[0] // 2
    y = lax.conv_general_dilated(x, w[:, None, :].astype(x.dtype), (1,), [(pad, pad)],
                                 dimension_numbers=('NWC', 'WIO', 'NWC'),
                                 feature_group_count=x.shape[-1])
    return y + b.astype(x.dtype)


def axial_rope(rows):
    row = jnp.repeat(jnp.arange(rows), GRID_W)
    col = jnp.tile(jnp.arange(GRID_W), rows)
    quarter = DA_HEAD_DIM // 4
    inv = ROPE_BASE ** (-jnp.arange(quarter, dtype=F32) / quarter)
    ang = jnp.stack([row, col], axis=-1).astype(F32)[..., None] * inv
    return jnp.cos(ang), jnp.sin(ang)


def apply_rope(x, cos, sin):
    shp = x.shape
    xs = x.astype(F32).reshape(shp[:-1] + (2, 2, DA_HEAD_DIM // 4))
    x1, x2 = xs[..., 0, :], xs[..., 1, :]
    c = cos[:, None, None]
    s = sin[:, None, None]
    out = jnp.stack([x1 * c - x2 * s, x2 * c + x1 * s], axis=-2)
    return out.reshape(shp).astype(x.dtype)


def ssd_prepare(u, conv_w, conv_b, dt_bias):
    b, n, _ = u.shape
    z = u[..., :SSD_INNER]
    xbc = jax.nn.silu(dwconv(u[..., SSD_INNER:SSD_INNER + SSD_CONV_CH], conv_w, conv_b))
    dt = jax.nn.softplus(u[..., SSD_INNER + SSD_CONV_CH:].astype(F32).reshape(b, n, 2, SSD_HEADS)
                         + dt_bias.astype(F32))
    gn = SSD_GROUPS * SSD_STATE
    xs = xbc[..., :SSD_INNER].reshape(b, n, SSD_HEADS, SSD_HEAD_DIM)
    bm = xbc[..., SSD_INNER:SSD_INNER + gn].reshape(b, n, SSD_GROUPS, SSD_STATE)
    cm = xbc[..., SSD_INNER + gn:].reshape(b, n, SSD_GROUPS, SSD_STATE)
    return z, xs, dt, bm, cm


def ssd_scan(x, dt, a, bm, cm, h0, with_output):
    b, l, _, p = x.shape
    g, n = bm.shape[2], bm.shape[3]
    r = SSD_HEADS // g
    nc, cs = l // SSD_CHUNK, SSD_CHUNK
    xc = x.astype(F32).reshape(b, nc, cs, g, r, p)
    dtc = dt.reshape(b, nc, cs, g, r)
    bc = bm.astype(F32).reshape(b, nc, cs, g, n)
    cc = cm.astype(F32).reshape(b, nc, cs, g, n)
    a_cum = jnp.cumsum(dtc * a.astype(F32).reshape(g, r), axis=2)
    a_end = a_cum[:, :, -1]
    xdt = xc * dtc[..., None]
    states = jnp.einsum('bcjgn,bcjgrp->bcgrpn', bc,
                        xdt * jnp.exp(a_end[:, :, None] - a_cum)[..., None])

    def step(h, inp):
        st, dec = inp
        return h * dec[..., None, None] + st, h

    h_last, h_in = lax.scan(step, h0.astype(F32),
                            (jnp.moveaxis(states, 1, 0), jnp.moveaxis(jnp.exp(a_end), 1, 0)))
    if not with_output:
        return None, h_last
    h_in = jnp.moveaxis(h_in, 0, 1)
    seg = jnp.moveaxis(a_cum, 2, -1)
    lower_tri = jnp.tril(jnp.ones((cs, cs), dtype=bool))
    decay = jnp.exp(jnp.where(lower_tri, seg[..., :, None] - seg[..., None, :], -jnp.inf))
    cb = jnp.einsum('bcign,bcjgn->bcgij', cc, bc)
    y_diag = jnp.einsum('bcgrij,bcjgrp->bcigrp', cb[:, :, :, None] * decay, xdt)
    y_off = jnp.einsum('bcign,bcgrpn->bcigrp', cc, h_in) * jnp.exp(a_cum)[..., None]
    return (y_diag + y_off).reshape(b, l, g * r, p).astype(x.dtype), h_last


def ssd_bidir(xs, dt, a, bm, cm, h0f, h0b, with_output):
    flip = lambda t: jnp.flip(t, axis=1)
    yf, hf = ssd_scan(xs, dt[:, :, 0], a[0], bm, cm, h0f, with_output)
    yb, hb = ssd_scan(flip(xs), flip(dt[:, :, 1]), a[1], flip(bm), flip(cm), h0b, with_output)
    y = yf + flip(yb) if with_output else None
    return y, hf, hb


def ssd_output(y, xs, z, d_skip, norm_g):
    b, n = y.shape[:2]
    y = (y + xs * d_skip[:, None].astype(xs.dtype)).reshape(b, n, SSD_INNER)
    return rms_norm(y * jax.nn.silu(z), norm_g)


def hyena_filter_freq(n, w1, b1, f1, w2, b2, f2, w3):
    t = jnp.linspace(0.0, 1.0, n, dtype=F32)[:, None]
    w = (2.0 * math.pi / n) * jnp.arange(n, dtype=F32)[:, None]
    bands = jnp.linspace(1e-4, HY_BANDS - 1, HY_BANDS, dtype=F32)
    z = jnp.concatenate([t, jnp.cos(bands * w), -jnp.sin(bands * w)], axis=-1)
    hid = jnp.sin(f1.astype(F32) * (z @ w1.astype(F32) + b1.astype(F32)))
    hid = jnp.sin(f2.astype(F32) * (hid @ w2.astype(F32) + b2.astype(F32)))
    filt = (hid @ w3.astype(F32)).reshape(n, 2, HY_ORDER, HY_WIDTH)
    deltas = jnp.abs(jnp.linspace(math.log(HY_TARGET) / HY_SLOW, math.log(HY_TARGET) / HY_FAST,
                                  HY_WIDTH, dtype=F32))
    filt = filt * jnp.exp(-t * deltas)[:, None, None, :]
    fwd, bwd = filt[:, 0], filt[:, 1]
    k = jnp.concatenate([fwd, jnp.zeros_like(fwd[:1]), jnp.flip(bwd[1:], axis=0)], axis=0)
    return jnp.fft.rfft(k, axis=0)


def long_conv(u, kf, bias):
    n = u.shape[1]
    uf = u.astype(F32)
    y = jnp.fft.irfft(jnp.fft.rfft(uf, n=2 * n, axis=1) * kf, n=2 * n, axis=1)[:, :n]
    return (y + uf * bias.astype(F32)).astype(u.dtype)


def hyena_mix(u, conv_w, conv_b, kf, bias):
    p = dwconv(u, conv_w, conv_b)
    v, x1, x2 = jnp.split(p, 3, axis=-1)
    z = x1 * long_conv(v, kf[:, 0], bias[0])
    return x2 * long_conv(z, kf[:, 1], bias[1])


def diff_lambda(lam_p, layer_idx):
    lam_init = 0.8 - 0.6 * math.exp(-0.3 * layer_idx)
    lp = lam_p.astype(F32)
    lam = jnp.exp(jnp.sum(lp[0] * lp[1])) - jnp.exp(jnp.sum(lp[2] * lp[3])) + lam_init
    return lam, lam_init


def da_split(u):
    b, n, _ = u.shape
    q = u[..., :DA_WIDTH].reshape(b, n, DA_HEADS, 2, DA_HEAD_DIM)
    k = u[..., DA_WIDTH:2 * DA_WIDTH].reshape(b, n, DA_HEADS, 2, DA_HEAD_DIM)
    v = u[..., 2 * DA_WIDTH:].reshape(b, n, DA_HEADS, 2 * DA_HEAD_DIM)
    return q, k, v


def diff_scores_mix(q, k, v, lam):
    s = jnp.einsum('bqhmd,bkhmd->bhmqk', q, k, preferred_element_type=F32) * (DA_HEAD_DIM ** -0.5)
    p = jax.nn.softmax(s, axis=-1)
    a = p[:, :, 0] - lam * p[:, :, 1]
    return jnp.einsum('bhqk,bkhe->bqhe', a.astype(v.dtype), v)


def diff_attn_latent(q, k, v, lam):
    b, n = q.shape[:2]
    nb = n // Q_BLOCK
    qb = jnp.moveaxis(q.reshape((b, nb, Q_BLOCK) + q.shape[2:]), 1, 0)
    out = lax.map(lambda qq: diff_scores_mix(qq, k, v, lam), qb)
    return jnp.moveaxis(out, 0, 1).reshape(b, n, DA_HEADS, 2 * DA_HEAD_DIM)


def da_post(o, g, lam_init):
    b, n = o.shape[:2]
    return (rms_norm(o, g) * (1.0 - lam_init)).reshape(b, n, DA_WIDTH)


def merge(branches, gate_logits, w_branch, w_out):
    g = jax.nn.sigmoid(gate_logits)
    mixed = g[..., :D_MODEL] * (branches[0] @ w_branch[0])
    mixed = mixed + g[..., D_MODEL:2 * D_MODEL] * (branches[1] @ w_branch[1])
    mixed = mixed + g[..., 2 * D_MODEL:] * (branches[2] @ w_branch[2])
    return mixed @ w_out


def conv_ffn(h, w_up, conv_w, conv_b, w_down):
    up = h @ w_up
    gate = dwconv(up[..., :D_FF], conv_w, conv_b)
    return (jax.nn.silu(gate) * up[..., D_FF:]) @ w_down


def token_mixer(h_c, h_l, cos, sin, layer_idx, ctx_out,
                w_in, ssd_conv_w, ssd_conv_b, ssd_a_log, ssd_dt_bias, ssd_d, ssd_norm,
                hy_conv_w, hy_conv_b, hy_filt, hy_bias, da_lambda, da_norm, w_branch, w_out):
    u_c = h_c @ w_in
    u_l = h_l @ w_in
    b = h_l.shape[0]
    a = -jnp.exp(ssd_a_log.astype(F32))
    zc, xc, dtc, bc, cc = ssd_prepare(u_c[..., :OFF_HY], ssd_conv_w, ssd_conv_b, ssd_dt_bias)
    h0 = jnp.zeros((b, SSD_GROUPS, SSD_HEADS // SSD_GROUPS, SSD_HEAD_DIM, SSD_STATE), F32)
    yc, hf, hb = ssd_bidir(xc, dtc, a, bc, cc, h0, h0, ctx_out)
    zl, xl, dtl, bl, cl = ssd_prepare(u_l[..., :OFF_HY], ssd_conv_w, ssd_conv_b, ssd_dt_bias)
    yl, _, _ = ssd_bidir(xl, dtl, a, bl, cl, hf, hb, True)
    ssd_l = ssd_output(yl, xl, zl, ssd_d, ssd_norm)
    hy_l = hyena_mix(u_l[..., OFF_HY:OFF_DA], hy_conv_w, hy_conv_b,
                     hyena_filter_freq(u_l.shape[1], *hy_filt), hy_bias)
    lam, lam_init = diff_lambda(da_lambda, layer_idx)
    qc, kc, vc = da_split(u_c[..., OFF_DA:OFF_GATE])
    ql, kl, vl = da_split(u_l[..., OFF_DA:OFF_GATE])
    ql = apply_rope(ql, cos, sin)
    kl = apply_rope(kl, cos, sin)
    k_all = jnp.concatenate([kc, kl], axis=1)
    v_all = jnp.concatenate([vc, vl], axis=1)
    da_l = da_post(diff_attn_latent(ql, k_all, v_all, lam), da_norm, lam_init)
    out_l = merge((ssd_l, hy_l, da_l), u_l[..., OFF_GATE:], w_branch, w_out)
    if not ctx_out:
        return None, out_l
    ssd_c = ssd_output(yc, xc, zc, ssd_d, ssd_norm)
    hy_c = hyena_mix(u_c[..., OFF_HY:OFF_DA], hy_conv_w, hy_conv_b,
                     hyena_filter_freq(u_c.shape[1], *hy_filt), hy_bias)
    da_c = da_post(diff_scores_mix(qc, kc, vc, lam), da_norm, lam_init)
    out_c = merge((ssd_c, hy_c, da_c), u_c[..., OFF_GATE:], w_branch, w_out)
    return out_c, out_l


def setup_inputs(seed: int = 0) -> dict:
    key = jax.random.key(seed)
    ks = iter(jax.random.split(key, 48))
    nrm = lambda shape, std: std * jax.random.normal(next(ks), shape, F32)
    gain = lambda shape: 1.0 + nrm(shape, 0.05)
    L = DEPTH
    dt0 = jnp.exp(jax.random.uniform(next(ks), (L, 2, SSD_HEADS), F32,
                                     minval=math.log(1e-3), maxval=math.log(1e-1)))
    return {
        "x": nrm((BATCH, SEQ, D_MODEL), 1.0),
        "c": nrm((BATCH, D_MODEL), 1.0),
        "ctx": nrm((BATCH, CTX_LEN, D_MODEL), 1.0),
        "c_ctx": nrm((D_MODEL,), 1.0),
        "w_ada": nrm((L, D_MODEL, 6 * D_MODEL), 0.5 * D_MODEL ** -0.5),
        "b_ada": nrm((L, 6 * D_MODEL), 0.01),
        "norm_g": gain((L, 4, D_MODEL)),
        "w_in": nrm((L, D_MODEL, IN_COLS), D_MODEL ** -0.5),
        "ssd_conv_w": nrm((L, SSD_CONV, SSD_CONV_CH), SSD_CONV ** -0.5),
        "ssd_conv_b": nrm((L, SSD_CONV_CH), 0.01),
        "ssd_a_log": jnp.log(jax.random.uniform(next(ks), (L, 2, SSD_HEADS), F32, minval=1.0, maxval=16.0)),
        "ssd_dt_bias": dt0 + jnp.log(-jnp.expm1(-dt0)),
        "ssd_d": gain((L, SSD_HEADS)),
        "ssd_norm": gain((L, SSD_INNER)),
        "hy_conv_w": nrm((L, HY_SHORT, IN_HY), HY_SHORT ** -0.5),
        "hy_conv_b": nrm((L, IN_HY), 0.01),
        "hy_w1": nrm((L, HY_EMB, HY_FF), HY_EMB ** -0.5),
        "hy_b1": nrm((L, HY_FF), 0.01),
        "hy_f1": gain((L, HY_FF)),
        "hy_w2": nrm((L, HY_FF, HY_FF), HY_FF ** -0.5),
        "hy_b2": nrm((L, HY_FF), 0.01),
        "hy_f2": gain((L, HY_FF)),
        "hy_w3": nrm((L, HY_FF, 2 * HY_ORDER * HY_WIDTH), 0.02),
        "hy_bias": nrm((L, HY_ORDER, HY_WIDTH), 1.0),
        "da_lambda": nrm((L, 4, DA_HEAD_DIM), 0.1),
        "da_norm": gain((L, 2 * DA_HEAD_DIM)),
        "w_branch": nrm((L, N_BRANCH, BRANCH_W, D_MODEL), BRANCH_W ** -0.5),
        "w_out": nrm((L, D_MODEL, D_MODEL), D_MODEL ** -0.5),
        "ffn_w_up": nrm((L, D_MODEL, 2 * D_FF), D_MODEL ** -0.5),
        "ffn_conv_w": nrm((L, FFN_CONV, D_FF), FFN_CONV ** -0.5),
        "ffn_conv_b": nrm((L, D_FF), 0.01),
        "ffn_w_down": nrm((L, D_FF, D_MODEL), D_FF ** -0.5),
    }


def reference(x, c, ctx, c_ctx, w_ada, b_ada, norm_g, w_in, ssd_conv_w, ssd_conv_b, ssd_a_log,
              ssd_dt_bias, ssd_d, ssd_norm, hy_conv_w, hy_conv_b, hy_w1, hy_b1, hy_f1, hy_w2, hy_b2,
              hy_f2, hy_w3, hy_bias, da_lambda, da_norm, w_branch, w_out, ffn_w_up, ffn_conv_w,
              ffn_conv_b, ffn_w_down):
    ROWS = x.shape[1] // GRID_W
    cos, sin = axial_rope(ROWS)
    x_l, x_c = x, ctx
    s_l = jax.nn.silu(c)
    s_c = jax.nn.silu(c_ctx)
    for i in range(DEPTH):
        ctx_out = i < DEPTH - 1
        mod_l = jnp.split((s_l @ w_ada[i] + b_ada[i])[:, None, :], 6, axis=-1)
        mod_c = jnp.split(s_c @ w_ada[i] + b_ada[i], 6, axis=-1)
        ng = norm_g[i]
        h_l = rms_norm(x_l, ng[0]) * (1.0 + mod_l[1]) + mod_l[0]
        h_c = rms_norm(x_c, ng[0]) * (1.0 + mod_c[1]) + mod_c[0]
        hy_filt = (hy_w1[i], hy_b1[i], hy_f1[i], hy_w2[i], hy_b2[i], hy_f2[i], hy_w3[i])
        out_c, out_l = token_mixer(h_c, h_l, cos, sin, i, ctx_out,
                                   w_in[i], ssd_conv_w[i], ssd_conv_b[i], ssd_a_log[i], ssd_dt_bias[i],
                                   ssd_d[i], ssd_norm[i], hy_conv_w[i], hy_conv_b[i], hy_filt, hy_bias[i],
                                   da_lambda[i], da_norm[i], w_branch[i], w_out[i])
        x_l = x_l + mod_l[2] * rms_norm(out_l, ng[1])
        f_l = conv_ffn(rms_norm(x_l, ng[2]) * (1.0 + mod_l[4]) + mod_l[3],
                       ffn_w_up[i], ffn_conv_w[i], ffn_conv_b[i], ffn_w_down[i])
        x_l = x_l + mod_l[5] * rms_norm(f_l, ng[3])
        if ctx_out:
            x_c = x_c + mod_c[2] * rms_norm(out_c, ng[1])
            f_c = conv_ffn(rms_norm(x_c, ng[2]) * (1.0 + mod_c[4]) + mod_c[3],
                           ffn_w_up[i], ffn_conv_w[i], ffn_conv_b[i], ffn_w_down[i])
            x_c = x_c + mod_c[5] * rms_norm(f_c, ng[3])
    return x_l
```

```python
import functools
import math

import numpy as np
import jax
import jax.numpy as jnp
from jax import lax
from jax.experimental import pallas as pl
from jax.experimental.pallas import tpu as pltpu

F32 = jnp.float32
BF16 = jnp.bfloat16
HIGHEST = lax.Precision.HIGHEST

D_MODEL = 1024
GRID_W = 64
EPS = 1e-6

SSD_INNER = D_MODEL
SSD_HEAD_DIM = 64
SSD_HEADS = SSD_INNER // SSD_HEAD_DIM
SSD_GROUPS = 2
SSD_STATE = 128
SSD_CONV = 5
SSD_CHUNK = 128
SSD_CONV_CH = SSD_INNER + 2 * SSD_GROUPS * SSD_STATE
SSD_GROUP_W = SSD_INNER // SSD_GROUPS

HY_WIDTH = D_MODEL
HY_ORDER = 2
HY_SHORT = 3
HY_BANDS = 16
HY_EMB = 1 + 2 * HY_BANDS
HY_FF = 64
HY_TARGET = 1e-2
HY_FAST = 0.3
HY_SLOW = 1.5
HY_N2 = 64
HY_CB = 128
HY_CB1 = 512
HY_PAD = 8

DA_HEAD_DIM = 64
DA_HEADS = D_MODEL // (2 * DA_HEAD_DIM)
DA_WIDTH = DA_HEADS * 2 * DA_HEAD_DIM
ROPE_BASE = 10000.0

D_FF = ((8 * D_MODEL // 3 + 127) // 128) * 128
FFN_CONV = 3

IN_SSD = SSD_INNER + SSD_CONV_CH + 2 * SSD_HEADS
IN_HY = (HY_ORDER + 1) * HY_WIDTH
IN_DA = 3 * DA_WIDTH
OFF_HY = IN_SSD
OFF_DA = OFF_HY + IN_HY
OFF_GATE = OFF_DA + IN_DA

BLK_Z, BLK_HY, BLK_Q, BLK_K, BLK_V, BLK_GATE = 0, 1, 4, 5, 6, 7
BIG_COLS = 10 * D_MODEL

VMEM_LIMIT = 56 * 1024 * 1024
LANES = 128
HALO = 16


def _cparams(*sem):
    return pltpu.CompilerParams(dimension_semantics=sem, vmem_limit_bytes=VMEM_LIMIT)


def _silu(x):
    return x * jax.nn.sigmoid(x)


def _rms(x, g):
    return x * lax.rsqrt(jnp.mean(x * x, axis=-1, keepdims=True) + EPS) * g


def _round_up(a, m):
    return (a + m - 1) // m * m


def _ada_body(s_ref, w_ref, b_ref, o_ref):
    s = _silu(s_ref[...])
    o_ref[0] = jnp.dot(s, w_ref[0], preferred_element_type=F32, precision=HIGHEST) + b_ref[0]


def _ada_mod(c, c_ctx, w_ada, b_ada):
    depth, d, n6 = w_ada.shape
    b = c.shape[0]
    rows = _round_up(b + 1, 8)
    s = jnp.zeros((rows, d), F32).at[:b].set(c).at[b].set(c_ctx)
    tn = 1536
    return pl.pallas_call(
        _ada_body,
        grid=(depth, n6 // tn),
        in_specs=[pl.BlockSpec((rows, d), lambda l, j: (0, 0)),
                  pl.BlockSpec((1, d, tn), lambda l, j: (l, 0, j)),
                  pl.BlockSpec((1, 1, tn), lambda l, j: (l, 0, j))],
        out_specs=pl.BlockSpec((1, rows, tn), lambda l, j: (l, 0, j)),
        out_shape=jax.ShapeDtypeStruct((depth, rows, n6), F32),
        compiler_params=_cparams("parallel", "parallel"),
        name="ada_mod",
    )(s, w_ada, b_ada.reshape(depth, 1, n6))


def _nmm_body(x_ref, g_ref, sh_ref, sc_ref, w_ref, o_ref, h_scr):
    @pl.when(pl.program_id(2) == 0)
    def _():
        h = _rms(x_ref[0], g_ref[...]) * (1.0 + sc_ref[0]) + sh_ref[0]
        h_scr[...] = h.astype(BF16)

    o_ref[0] = jnp.dot(h_scr[...], w_ref[...], preferred_element_type=F32).astype(o_ref.dtype)


def _norm_mod_matmul(x, g, shift, scale, w, tn, out_dtype, name):
    b, l, d = x.shape
    n = w.shape[1]
    tm = min(l, 1024)
    return pl.pallas_call(
        _nmm_body,
        grid=(b, l // tm, n // tn),
        in_specs=[pl.BlockSpec((1, tm, d), lambda i, m, j: (i, m, 0)),
                  pl.BlockSpec((1, d), lambda i, m, j: (0, 0)),
                  pl.BlockSpec((1, 1, d), lambda i, m, j: (i, 0, 0)),
                  pl.BlockSpec((1, 1, d), lambda i, m, j: (i, 0, 0)),
                  pl.BlockSpec((d, tn), lambda i, m, j: (0, j))],
        out_specs=pl.BlockSpec((1, tm, tn), lambda i, m, j: (i, m, j)),
        out_shape=jax.ShapeDtypeStruct((b, l, n), out_dtype),
        scratch_shapes=[pltpu.VMEM((tm, d), BF16)],
        compiler_params=_cparams("parallel", "parallel", "arbitrary"),
        name=name,
    )(x, g.reshape(1, d), shift.reshape(b, 1, d), scale.reshape(b, 1, d), w)


def _halo_specs(tl, l, c, col):
    per = tl // HALO
    last = l // HALO - 1
    prev = pl.BlockSpec((1, HALO, c), lambda i, m: (i, jnp.maximum(m * per - 1, 0), col))
    nxt = pl.BlockSpec((1, HALO, c), lambda i, m: (i, jnp.minimum((m + 1) * per, last), col))
    return prev, nxt


def _conv_rows(x, prev, nxt, w, bias, m, nm):
    tl = x.shape[0]
    k = w.shape[0]
    pad = k // 2
    prev = prev * (m > 0).astype(F32)
    nxt = nxt * (m < nm - 1).astype(F32)
    ext = jnp.concatenate([prev, x, nxt], axis=0)
    rows = tl + 2 * HALO
    y = bias
    for j in range(k):
        shifted = ext if j == pad else pltpu.roll(ext, (pad - j) % rows, 0)
        y = y + shifted[HALO:HALO + tl] * w[j:j + 1]
    return y


def _ssd_prep_body(x_ref, p_ref, n_ref, w_ref, b_ref, o_ref):
    m, nm = pl.program_id(1), pl.num_programs(1)
    y = _conv_rows(x_ref[0].astype(F32), p_ref[0].astype(F32), n_ref[0].astype(F32),
                   w_ref[...], b_ref[...], m, nm)
    o_ref[0] = _silu(y).astype(o_ref.dtype)


def _ssd_prep(xbc_raw, conv_w, conv_b):
    b, l, c = xbc_raw.shape
    tl = min(l, 512)
    prev, nxt = _halo_specs(tl, l, c, 0)
    return pl.pallas_call(
        _ssd_prep_body,
        grid=(b, l // tl),
        in_specs=[pl.BlockSpec((1, tl, c), lambda i, m: (i, m, 0)), prev, nxt,
                  pl.BlockSpec((SSD_CONV, c), lambda i, m: (0, 0)),
                  pl.BlockSpec((1, c), lambda i, m: (0, 0))],
        out_specs=pl.BlockSpec((1, tl, c), lambda i, m: (i, m, 0)),
        out_shape=jax.ShapeDtypeStruct((b, l, c), BF16),
        compiler_params=_cparams("parallel", "parallel"),
        name="ssd_prep",
    )(xbc_raw, xbc_raw, xbc_raw, conv_w, conv_b.reshape(1, c))


def _softplus(x):
    return jnp.maximum(x, 0.0) + jnp.log1p(jnp.exp(-jnp.abs(x)))


def _ssd_chunk(xbc, dt_raw, dtb, alog, h_scr, reverse, col0):
    cs = xbc.shape[0]
    gw = SSD_GROUPS * SSD_STATE
    x = xbc[:, :SSD_INNER].astype(F32)
    dt = _softplus(dt_raw + dtb)
    da = dt * (-jnp.exp(alog))
    row = lax.broadcasted_iota(jnp.int32, (cs, cs), 0)
    col = lax.broadcasted_iota(jnp.int32, (cs, cs), 1)
    keep = (row <= col) if reverse else (row >= col)
    acum = jnp.dot(keep.astype(F32), da, preferred_element_type=F32, precision=HIGHEST)
    acum_t = acum.T
    head_of_lane = col0 + lax.broadcasted_iota(jnp.int32, (LANES, SSD_INNER), 1) // SSD_HEAD_DIM
    expand = (lax.broadcasted_iota(jnp.int32, (LANES, SSD_INNER), 0) == head_of_lane).astype(F32)
    acum_x = jnp.dot(acum, expand, preferred_element_type=F32, precision=HIGHEST)
    dt_x = jnp.dot(dt, expand, preferred_element_type=F32, precision=HIGHEST)
    end = 0 if reverse else cs - 1
    aend_x = acum_x[end:end + 1]
    xdt = x * dt_x
    xw = (xdt * jnp.exp(aend_x - acum_x)).astype(BF16)
    into = jnp.exp(acum_x)
    dec = jnp.exp(aend_x)
    xdt_b = xdt.astype(BF16)
    lane = lax.broadcasted_iota(jnp.int32, (cs, LANES), 1)
    parts = []
    for g in range(SSD_GROUPS):
        bm = xbc[:, SSD_INNER + g * SSD_STATE:SSD_INNER + (g + 1) * SSD_STATE]
        cm = xbc[:, SSD_INNER + gw + g * SSD_STATE:SSD_INNER + gw + (g + 1) * SSD_STATE]
        cb = lax.dot_general(cm, bm, (((1,), (1,)), ((), ())), preferred_element_type=F32)
        sl = slice(g * SSD_GROUP_W, (g + 1) * SSD_GROUP_W)
        h_t = h_scr[g]
        y_off = jnp.dot(cm, h_t.astype(BF16), preferred_element_type=F32) * into[:, sl]
        bm_t = bm.astype(F32).T.astype(BF16)
        h_scr[g] = h_t * dec[:, sl] + jnp.dot(bm_t, xw[:, sl], preferred_element_type=F32)
        for p in range(SSD_GROUP_W // LANES):
            lo = g * SSD_GROUP_W + p * LANES
            xp = xdt_b[:, lo:lo + LANES]
            ys = []
            for q in range(LANES // SSD_HEAD_DIM):
                hc = col0 + lo // SSD_HEAD_DIM + q
                seg = acum[:, hc:hc + 1] - acum_t[hc:hc + 1, :]
                dmat = (cb * jnp.exp(jnp.where(keep, seg, -1e30))).astype(BF16)
                ys.append(jnp.dot(dmat, xp, preferred_element_type=F32))
            y_diag = jnp.where(lane < SSD_HEAD_DIM, ys[0], ys[1])
            parts.append(y_diag + y_off[:, p * LANES:(p + 1) * LANES])
    return jnp.concatenate(parts, axis=1), x


def _ssd_fwd_body(xbc_ref, dt_ref, dtb_ref, alog_ref, h0_ref, y_ref, hl_ref, h_scr):
    c = pl.program_id(1)

    @pl.when(c == 0)
    def _():
        h_scr[...] = h0_ref[0]

    y, _ = _ssd_chunk(xbc_ref[0], dt_ref[0], dtb_ref[...], alog_ref[...], h_scr, False, 0)
    y_ref[0] = y

    @pl.when(c == pl.num_programs(1) - 1)
    def _():
        hl_ref[0] = h_scr[...]


def _ssd_bwd_body(xbc_ref, dt_ref, dtb_ref, alog_ref, h0_ref, yf_ref, z_ref, dsk_ref, ng_ref,
                  o_ref, hl_ref, h_scr):
    c = pl.program_id(1)

    @pl.when(c == 0)
    def _():
        h_scr[...] = h0_ref[0]

    yb, x = _ssd_chunk(xbc_ref[0], dt_ref[0], dtb_ref[...], alog_ref[...], h_scr, True, SSD_HEADS)
    y = yf_ref[0] + yb + x * dsk_ref[...]
    o_ref[0] = _rms(y * _silu(z_ref[0].astype(F32)), ng_ref[...]).astype(o_ref.dtype)

    @pl.when(c == pl.num_programs(1) - 1)
    def _():
        hl_ref[0] = h_scr[...]


def _ssd_branch(xbc_act, dt_raw, big, dtb, alog, dskip_x, norm_g, h0f, h0b):
    b, l, _ = xbc_act.shape
    cs = SSD_CHUNK
    nc = l // cs
    hshape = (SSD_GROUPS, SSD_STATE, SSD_GROUP_W)
    state_spec = pl.BlockSpec((1,) + hshape, lambda i, c: (i, 0, 0, 0))
    vec = lambda w: pl.BlockSpec((1, w), lambda i, c: (0, 0))
    state_shape = jax.ShapeDtypeStruct((b,) + hshape, F32)
    yf, hf = pl.pallas_call(
        _ssd_fwd_body,
        grid=(b, nc),
        in_specs=[pl.BlockSpec((1, cs, SSD_CONV_CH), lambda i, c: (i, c, 0)),
                  pl.BlockSpec((1, cs, LANES), lambda i, c: (i, c, 0)),
                  vec(LANES), vec(LANES), state_spec],
        out_specs=[pl.BlockSpec((1, cs, SSD_INNER), lambda i, c: (i, c, 0)), state_spec],
        out_shape=[jax.ShapeDtypeStruct((b, l, SSD_INNER), F32), state_shape],
        scratch_shapes=[pltpu.VMEM(hshape, F32)],
        compiler_params=_cparams("parallel", "arbitrary"),
        name="ssd_scan_fwd",
    )(xbc_act, dt_raw, dtb, alog, h0f)
    rev = lambda i, c: (i, nc - 1 - c, 0)
    out, hb = pl.pallas_call(
        _ssd_bwd_body,
        grid=(b, nc),
        in_specs=[pl.BlockSpec((1, cs, SSD_CONV_CH), rev),
                  pl.BlockSpec((1, cs, LANES), rev),
                  vec(LANES), vec(LANES), state_spec,
                  pl.BlockSpec((1, cs, SSD_INNER), rev),
                  pl.BlockSpec((1, cs, SSD_INNER), lambda i, c: (i, nc - 1 - c, BLK_Z)),
                  vec(SSD_INNER), vec(SSD_INNER)],
        out_specs=[pl.BlockSpec((1, cs, SSD_INNER), rev), state_spec],
        out_shape=[jax.ShapeDtypeStruct((b, l, SSD_INNER), BF16), state_shape],
        scratch_shapes=[pltpu.VMEM(hshape, F32)],
        compiler_params=_cparams("parallel", "arbitrary"),
        name="ssd_scan_bwd",
    )(xbc_act, dt_raw, dtb, alog, h0b, yf, big, dskip_x, norm_g.reshape(1, SSD_INNER))
    return out, hf, hb


def _hy_filter_body(z_ref, w1_ref, b1_ref, f1_ref, w2_ref, b2_ref, f2_ref, w3_ref, dl_ref, o_ref, *, n, tr):
    z = z_ref[...]
    h = jnp.sin(f1_ref[...] * (jnp.dot(z, w1_ref[...], preferred_element_type=F32, precision=HIGHEST)
                               + b1_ref[...]))
    h = jnp.sin(f2_ref[...] * (jnp.dot(h, w2_ref[...], preferred_element_type=F32, precision=HIGHEST)
                               + b2_ref[...]))
    filt = jnp.dot(h, w3_ref[...], preferred_element_type=F32, precision=HIGHEST)
    filt = filt * jnp.exp(-z[:, 0:1] * dl_ref[...])
    rows = pl.program_id(0) * tr + lax.broadcasted_iota(jnp.int32, (tr, 1), 0)
    o_ref[...] = jnp.where(rows == n, 0.0, filt)


def _hy_features(n):
    t = jnp.linspace(0.0, 1.0, n, dtype=F32)[:, None]
    w = (2.0 * math.pi / n) * jnp.arange(n, dtype=F32)[:, None]
    bands = jnp.linspace(1e-4, HY_BANDS - 1, HY_BANDS, dtype=F32)
    z = jnp.concatenate([t, jnp.cos(bands * w), -jnp.sin(bands * w)], axis=-1)
    return jnp.pad(z, ((0, 0), (0, LANES - HY_EMB)))


def _hy_filter(n, w1, b1, f1, w2, b2, f2, w3):
    z = _hy_features(n)
    zk = jnp.concatenate([z, z[:1], jnp.flip(z[1:], axis=0)], axis=0)
    padw = lambda a, r, c: jnp.pad(a.astype(F32), ((0, r - a.shape[0]), (0, c - a.shape[1])))
    w1p = padw(w1, LANES, LANES)
    w2p = padw(w2, LANES, LANES)
    w3p = padw(w3, LANES, w3.shape[1])
    rowp = lambda a: padw(a.reshape(1, -1), 1, LANES)
    deltas = jnp.abs(jnp.linspace(math.log(HY_TARGET) / HY_SLOW, math.log(HY_TARGET) / HY_FAST,
                                  HY_WIDTH, dtype=F32))
    dl = jnp.tile(deltas, HY_ORDER).reshape(1, HY_ORDER * HY_WIDTH)
    tr = min(n, 512)
    wide = HY_ORDER * HY_WIDTH
    full = lambda r, c: pl.BlockSpec((r, c), lambda i: (0, 0))
    return pl.pallas_call(
        functools.partial(_hy_filter_body, n=n, tr=tr),
        grid=(2 * n // tr,),
        in_specs=[pl.BlockSpec((tr, LANES), lambda i: (i, 0)),
                  full(LANES, LANES), full(1, LANES), full(1, LANES),
                  full(LANES, LANES), full(1, LANES), full(1, LANES),
                  pl.BlockSpec((LANES, wide), lambda i: (0, (i * tr) // n)),
                  full(1, wide)],
        out_specs=pl.BlockSpec((tr, wide), lambda i: (i, 0)),
        out_shape=jax.ShapeDtypeStruct((2 * n, wide), F32),
        compiler_params=_cparams("parallel"),
        name="hy_filter",
    )(zk, w1p, rowp(b1), rowp(f1), w2p, rowp(b2), rowp(f2), w3p, dl)


def _two_stage_tables(n, filt_rows):
    big_n = 2 * n
    n2 = HY_N2
    n1 = big_n // n2
    k1n = n1 // 2 + 1
    p = _round_up(k1n, 8)
    s2 = np.arange(n2)
    k1 = np.arange(k1n)

    def first(s_rows):
        s1 = np.arange(s_rows)
        ang = 2.0 * np.pi * ((k1[None, :, None] * (n2 * s1[None, None, :] + s2[:, None, None])) % big_n) / big_n
        out = np.zeros((n2, 2 * p, s_rows), np.float32)
        out[:, :k1n] = np.cos(ang)
        out[:, p:p + k1n] = -np.sin(ang)
        return out

    ang2 = 2.0 * np.pi * ((s2[:, None] * s2[None, :]) % n2) / n2
    c2, sn2 = np.cos(ang2), np.sin(ang2)
    second = np.block([[c2, sn2], [-sn2, c2]]).astype(np.float32)
    second_inv = np.block([[c2, -sn2], [sn2, c2]]).astype(np.float32)
    s1o = np.arange(n // n2)
    wgt = np.where((k1 == 0) | (k1 == n1 // 2), 1.0, 2.0) / big_n
    ang = 2.0 * np.pi * ((k1[None, None, :] * (n2 * s1o[None, :, None] + s2[:, None, None])) % big_n) / big_n
    last = np.zeros((n2, n // n2, 2 * p), np.float32)
    last[:, :, :k1n] = wgt * np.cos(ang)
    last[:, :, p:p + k1n] = -wgt * np.sin(ang)
    cast = lambda a: jnp.asarray(a).astype(BF16)
    return dict(k1n=k1n, p=p, first=cast(first(n // n2)), first_filt=cast(first(filt_rows // n2)),
                second=cast(second), second_inv=cast(second_inv), last=cast(last))


def _one_stage_tables(n):
    big_n = 2 * n
    kn = n + 1
    p = _round_up(kn, 8)
    k = np.arange(kn)

    def fwd(rows):
        s = np.arange(rows)
        ang = 2.0 * np.pi * ((k[:, None] * s[None, :]) % big_n) / big_n
        out = np.zeros((2 * p, rows), np.float32)
        out[:kn] = np.cos(ang)
        out[p:p + kn] = -np.sin(ang)
        return out

    s = np.arange(n)
    wgt = np.where((k == 0) | (k == n), 1.0, 2.0) / big_n
    ang = 2.0 * np.pi * ((s[:, None] * k[None, :]) % big_n) / big_n
    inv = np.zeros((n, 2 * p), np.float32)
    inv[:, :kn] = wgt * np.cos(ang)
    inv[:, p:p + kn] = -wgt * np.sin(ang)
    cast = lambda a: jnp.asarray(a).astype(BF16)
    return dict(p=p, fwd=cast(fwd(n)), fwd_filt=cast(fwd(big_n)), inv=cast(inv))


def _spec2_body(k_ref, f1_ref, f2_ref, o_ref, a_scr, *, k1n, p, s_rows):
    n2 = HY_N2

    def stage1(s2, carry):
        rows = k_ref[pl.ds(s2, s_rows, stride=n2), :]
        a = jnp.dot(f1_ref[s2], rows.astype(BF16), preferred_element_type=F32)
        a_scr[pl.ds(pl.multiple_of(s2 * 2 * p, 8), 2 * p), :] = a
        return carry

    lax.fori_loop(0, n2, stage1, 0)

    def stage2(k1, carry):
        re = a_scr[pl.ds(k1, n2, stride=2 * p), :]
        im = a_scr[pl.ds(p + k1, n2, stride=2 * p), :]
        a = jnp.concatenate([re, im], axis=0).astype(BF16)
        o_ref[k1] = jnp.dot(f2_ref[...], a, preferred_element_type=F32)
        return carry

    lax.fori_loop(0, k1n, stage2, 0)


def _hy_spectrum2(kfilt, tabs):
    rows, wide = kfilt.shape
    k1n, p = tabs["k1n"], tabs["p"]
    s_rows = rows // HY_N2
    cb = HY_CB
    return pl.pallas_call(
        functools.partial(_spec2_body, k1n=k1n, p=p, s_rows=s_rows),
        grid=(wide // cb,),
        in_specs=[pl.BlockSpec((rows, cb), lambda j: (0, j)),
                  pl.BlockSpec((HY_N2, 2 * p, s_rows), lambda j: (0, 0, 0)),
                  pl.BlockSpec((2 * HY_N2, 2 * HY_N2), lambda j: (0, 0))],
        out_specs=pl.BlockSpec((k1n, 2 * HY_N2, cb), lambda j: (0, 0, j)),
        out_shape=jax.ShapeDtypeStruct((k1n, 2 * HY_N2, wide), F32),
        scratch_shapes=[pltpu.VMEM((HY_N2 * 2 * p, cb), F32)],
        compiler_params=_cparams("parallel"),
        name="hy_spectrum2",
    )(kfilt, tabs["first_filt"], tabs["second"])


def _spec1_body(k_ref, f_ref, o_ref):
    o_ref[...] = jnp.dot(f_ref[...], k_ref[...].astype(BF16), preferred_element_type=F32)


def _hy_spectrum1(kfilt, tabs):
    rows, wide = kfilt.shape
    p = tabs["p"]
    cb = HY_CB1
    return pl.pallas_call(
        _spec1_body,
        grid=(wide // cb,),
        in_specs=[pl.BlockSpec((rows, cb), lambda j: (0, j)),
                  pl.BlockSpec((2 * p, rows), lambda j: (0, 0))],
        out_specs=pl.BlockSpec((2 * p, cb), lambda j: (0, j)),
        out_shape=jax.ShapeDtypeStruct((2 * p, wide), F32),
        compiler_params=_cparams("parallel"),
        name="hy_spectrum1",
    )(kfilt, tabs["fwd_filt"])


def _short_conv_seq(src_ref, w_ref, b_ref, pad_scr, dst_scr, n, rb):
    for r in range(0, n, rb):
        pad_scr[pl.ds(HY_PAD + r, rb), :] = src_ref[0, pl.ds(r, rb), :].astype(F32)
    for r in range(0, n, rb):
        acc = b_ref[...]
        for j in range(HY_SHORT):
            acc = acc + pad_scr[pl.ds(HY_PAD - 1 + j + r, rb), :] * w_ref[j:j + 1, :]
        dst_scr[pl.ds(r, rb), :] = acc


def _zero_pad_rows(pad_scr, n):
    zeros = jnp.zeros((HY_PAD, pad_scr.shape[1]), F32)
    pad_scr[pl.ds(0, HY_PAD), :] = zeros
    pad_scr[pl.ds(HY_PAD + n, HY_PAD), :] = zeros


def _load_u(u_ref, cwu_ref, cbu_ref, pad_scr, u_scr, n, rb, conv_u):
    if conv_u:
        _short_conv_seq(u_ref, cwu_ref, cbu_ref, pad_scr, u_scr, n, rb)
    else:
        for r in range(0, n, rb):
            u_scr[pl.ds(r, rb), :] = u_ref[0, pl.ds(r, rb), :].astype(F32)


def _gate_out(g_ref, cwg_ref, cbg_ref, bias_ref, pad_scr, u_scr, y_scr, o_ref, n, rb):
    for r in range(0, n, rb):
        pad_scr[pl.ds(HY_PAD + r, rb), :] = g_ref[0, pl.ds(r, rb), :].astype(F32)
    for r in range(0, n, rb):
        gate = cbg_ref[...]
        for j in range(HY_SHORT):
            gate = gate + pad_scr[pl.ds(HY_PAD - 1 + j + r, rb), :] * cwg_ref[j:j + 1, :]
        y = y_scr[pl.ds(r, rb), :] + u_scr[pl.ds(r, rb), :] * bias_ref[...]
        o_ref[0, pl.ds(r, rb), :] = (gate * y).astype(o_ref.dtype)


def _conv2_body(u_ref, g_ref, cwu_ref, cbu_ref, cwg_ref, cbg_ref, bias_ref, kf_ref,
                f1_ref, f2_ref, i1_ref, l_ref, o_ref, pad_scr, u_scr, y_scr, a_scr, b_scr,
                *, n, k1n, p, conv_u):
    n2 = HY_N2
    s_rows = n // n2
    rb = min(n, 512)
    _zero_pad_rows(pad_scr, n)
    _load_u(u_ref, cwu_ref, cbu_ref, pad_scr, u_scr, n, rb, conv_u)

    def stage1(s2, carry):
        rows = u_scr[pl.ds(s2, s_rows, stride=n2), :]
        a = jnp.dot(f1_ref[s2], rows.astype(BF16), preferred_element_type=F32)
        a_scr[pl.ds(pl.multiple_of(s2 * 2 * p, 8), 2 * p), :] = a
        return carry

    lax.fori_loop(0, n2, stage1, 0)

    if p > k1n:
        b_scr[pl.ds(k1n * 2 * n2, (p - k1n) * 2 * n2), :] = jnp.zeros(((p - k1n) * 2 * n2, b_scr.shape[1]), F32)

    def stage2(k1, carry):
        re = a_scr[pl.ds(k1, n2, stride=2 * p), :]
        im = a_scr[pl.ds(p + k1, n2, stride=2 * p), :]
        x = jnp.dot(f2_ref[...], jnp.concatenate([re, im], axis=0).astype(BF16), preferred_element_type=F32)
        kf = kf_ref[k1]
        xr, xi, kr, ki = x[:n2], x[n2:], kf[:n2], kf[n2:]
        prod = jnp.concatenate([xr * kr - xi * ki, xr * ki + xi * kr], axis=0).astype(BF16)
        b_scr[pl.ds(pl.multiple_of(k1 * 2 * n2, 8), 2 * n2), :] = jnp.dot(
            i1_ref[...], prod, preferred_element_type=F32)
        return carry

    lax.fori_loop(0, k1n, stage2, 0)

    def stage3(s2, carry):
        re = b_scr[pl.ds(s2, p, stride=2 * n2), :]
        im = b_scr[pl.ds(n2 + s2, p, stride=2 * n2), :]
        y = jnp.dot(l_ref[s2], jnp.concatenate([re, im], axis=0).astype(BF16), preferred_element_type=F32)
        y_scr[pl.ds(s2, s_rows, stride=n2), :] = y
        return carry

    lax.fori_loop(0, n2, stage3, 0)
    _gate_out(g_ref, cwg_ref, cbg_ref, bias_ref, pad_scr, u_scr, y_scr, o_ref, n, rb)


def _conv1_body(u_ref, g_ref, cwu_ref, cbu_ref, cwg_ref, cbg_ref, bias_ref, kf_ref,
                f_ref, inv_ref, o_ref, pad_scr, u_scr, y_scr, *, n, p, conv_u):
    rb = n
    _zero_pad_rows(pad_scr, n)
    _load_u(u_ref, cwu_ref, cbu_ref, pad_scr, u_scr, n, rb, conv_u)
    x = jnp.dot(f_ref[...], u_scr[...].astype(BF16), preferred_element_type=F32)
    kf = kf_ref[...]
    xr, xi, kr, ki = x[:p], x[p:], kf[:p], kf[p:]
    prod = jnp.concatenate([xr * kr - xi * ki, xr * ki + xi * kr], axis=0).astype(BF16)
    y_scr[...] = jnp.dot(inv_ref[...], prod, preferred_element_type=F32)
    _gate_out(g_ref, cwg_ref, cbg_ref, bias_ref, pad_scr, u_scr, y_scr, o_ref, n, rb)


def _hy_long_conv(u_arr, u_blk, conv_u, g_arr, g_blk, conv_w, conv_b, bias, kf, order, tabs, two_stage):
    b, n, _ = g_arr.shape
    c = HY_WIDTH
    cb = HY_CB if two_stage else HY_CB1
    per = c // cb
    ublk = u_blk * per
    gblk = g_blk * per
    wu = ((u_blk - BLK_HY) if conv_u else 0) * per
    wg = (g_blk - BLK_HY) * per
    seq = lambda off: pl.BlockSpec((1, n, cb), lambda j, i: (i, 0, off + j))
    wrow = lambda rows, off: pl.BlockSpec((rows, cb), lambda j, i: (0, off + j))
    const = lambda shape: pl.BlockSpec(shape, lambda j, i: (0,) * len(shape))
    common = [seq(ublk), seq(gblk), wrow(HY_SHORT, wu), wrow(1, wu), wrow(HY_SHORT, wg), wrow(1, wg),
              wrow(1, order * per)]
    args = [u_arr, g_arr, conv_w, conv_b, conv_w, conv_b, bias.reshape(1, HY_ORDER * c)]
    scratch = [pltpu.VMEM((n + 2 * HY_PAD, cb), F32), pltpu.VMEM((n, cb), F32), pltpu.VMEM((n, cb), F32)]
    if two_stage:
        k1n, p = tabs["k1n"], tabs["p"]
        n2 = HY_N2
        body = functools.partial(_conv2_body, n=n, k1n=k1n, p=p, conv_u=conv_u)
        specs = common + [pl.BlockSpec((k1n, 2 * n2, cb), lambda j, i: (0, 0, order * per + j)),
                          const((n2, 2 * p, n // n2)), const((2 * n2, 2 * n2)), const((2 * n2, 2 * n2)),
                          const((n2, n // n2, 2 * p))]
        args += [kf, tabs["first"], tabs["second"], tabs["second_inv"], tabs["last"]]
        scratch += [pltpu.VMEM((n2 * 2 * p, cb), F32), pltpu.VMEM((p * 2 * n2, cb), F32)]
        name = "hy_conv2"
    else:
        p = tabs["p"]
        body = functools.partial(_conv1_body, n=n, p=p, conv_u=conv_u)
        specs = common + [pl.BlockSpec((2 * p, cb), lambda j, i: (0, order * per + j)),
                          const((2 * p, n)), const((n, 2 * p))]
        args += [kf, tabs["fwd"], tabs["inv"]]
        name = "hy_conv1"
    return pl.pallas_call(
        body,
        grid=(per, b),
        in_specs=specs,
        out_specs=pl.BlockSpec((1, n, cb), lambda j, i: (i, 0, j)),
        out_shape=jax.ShapeDtypeStruct((b, n, c), BF16),
        scratch_shapes=scratch,
        compiler_params=_cparams("parallel", "parallel"),
        name=name,
    )(*args)


def _hyena_branch(big, conv_w, conv_b, bias, filt_params, two_stage):
    b, n, _ = big.shape
    kfilt = _hy_filter(n, *filt_params)
    if two_stage:
        tabs = _two_stage_tables(n, 2 * n)
        kf = _hy_spectrum2(kfilt, tabs)
    else:
        tabs = _one_stage_tables(n)
        kf = _hy_spectrum1(kfilt, tabs)
    cw = conv_w.astype(F32)
    cbias = conv_b.astype(F32).reshape(1, -1)
    z = _hy_long_conv(big, BLK_HY, True, big, BLK_HY + 1, cw, cbias, bias, kf, 0, tabs, two_stage)
    return _hy_long_conv(z, 0, False, big, BLK_HY + 2, cw, cbias, bias, kf, 1, tabs, two_stage)


def _rope_tables(n):
    rows = n // GRID_W
    row = jnp.repeat(jnp.arange(rows), GRID_W)
    col = jnp.tile(jnp.arange(GRID_W), rows)
    quarter = DA_HEAD_DIM // 4
    inv = ROPE_BASE ** (-jnp.arange(quarter, dtype=F32) / quarter)
    ang = jnp.stack([row, col], axis=-1).astype(F32)[..., None] * inv
    cos, sin = jnp.cos(ang), jnp.sin(ang)
    cos_h = jnp.concatenate([cos, cos], axis=-1).reshape(n, DA_HEAD_DIM)
    sin_h = jnp.concatenate([-sin, sin], axis=-1).reshape(n, DA_HEAD_DIM)
    return jnp.tile(cos_h, (1, 2)), jnp.tile(sin_h, (1, 2))


def _rope_body(x_ref, c_ref, s_ref, o_ref):
    quarter = DA_HEAD_DIM // 4
    cos, sin = c_ref[...], s_ref[...]
    lane = lax.broadcasted_iota(jnp.int32, cos.shape, 1)
    first_half = (lane % (2 * quarter)) < quarter
    for k in range(x_ref.shape[2] // LANES):
        x = x_ref[0, :, k * LANES:(k + 1) * LANES].astype(F32)
        partner = jnp.where(first_half, pltpu.roll(x, LANES - quarter, 1), pltpu.roll(x, quarter, 1))
        o_ref[0, :, k * LANES:(k + 1) * LANES] = (x * cos + partner * sin).astype(o_ref.dtype)


def _rope_qk(big, cos_t, sin_t):
    b, l, _ = big.shape
    tm = min(l, 512)
    return pl.pallas_call(
        _rope_body,
        grid=(b, l // tm, 2),
        in_specs=[pl.BlockSpec((1, tm, DA_WIDTH), lambda i, m, j: (i, m, BLK_Q + j)),
                  pl.BlockSpec((tm, LANES), lambda i, m, j: (m, 0)),
                  pl.BlockSpec((tm, LANES), lambda i, m, j: (m, 0))],
        out_specs=pl.BlockSpec((1, tm, DA_WIDTH), lambda i, m, j: (i, m, j)),
        out_shape=jax.ShapeDtypeStruct((b, l, 2 * DA_WIDTH), BF16),
        compiler_params=_cparams("parallel", "parallel", "parallel"),
        name="rope_qk",
    )(big, cos_t, sin_t)


def _attn_body(*refs, nseg, lam_init):
    lp_ref, g_ref, q_ref = refs[:3]
    k_refs = refs[3:3 + nseg]
    v_refs = refs[3 + nseg:3 + 2 * nseg]
    o_ref = refs[3 + 2 * nseg]
    lp = lp_ref[...]
    lam = (jnp.exp(jnp.sum(lp[0:1] * lp[1:2], axis=1, keepdims=True))
           - jnp.exp(jnp.sum(lp[2:3] * lp[3:4], axis=1, keepdims=True)) + lam_init)
    q = q_ref[0] * (DA_HEAD_DIM ** -0.5)
    lane = lax.broadcasted_iota(jnp.int32, q.shape, 1)
    zero = jnp.zeros_like(q)
    probs = []
    for mp in range(2):
        qm = jnp.where((lane // DA_HEAD_DIM) == mp, q, zero)
        s = [lax.dot_general(qm, k_ref[0], (((1,), (1,)), ((), ())), preferred_element_type=F32)
             for k_ref in k_refs]
        mx = functools.reduce(jnp.maximum, [jnp.max(t, axis=1, keepdims=True) for t in s])
        e = [jnp.exp(t - mx) for t in s]
        tot = functools.reduce(jnp.add, [jnp.sum(t, axis=1, keepdims=True) for t in e])
        probs.append((e, 1.0 / tot))
    (e0, r0), (e1, r1) = probs
    r1 = r1 * lam
    o = None
    for t0, t1, v_ref in zip(e0, e1, v_refs):
        a = (t0 * r0 - t1 * r1).astype(BF16)
        part = jnp.dot(a, v_ref[0], preferred_element_type=F32)
        o = part if o is None else o + part
    o_ref[0] = (_rms(o, g_ref[...]) * (1.0 - lam_init)).astype(o_ref.dtype)


def _diff_attention(q_arr, q_blk0, kv_segs, lam_p, norm_g, lam_init):
    b, lq, _ = q_arr.shape
    tq = min(lq, 256)
    nseg = len(kv_segs)
    hw = 2 * DA_HEAD_DIM
    specs = [pl.BlockSpec((4, DA_HEAD_DIM), lambda i, h, m: (0, 0)),
             pl.BlockSpec((1, hw), lambda i, h, m: (0, 0)),
             pl.BlockSpec((1, tq, hw), lambda i, h, m: (i, m, q_blk0 + h))]
    args = [lam_p.astype(F32), norm_g.reshape(1, hw), q_arr]
    for k_arr, k_blk0, _, _ in kv_segs:
        specs.append(pl.BlockSpec((1, k_arr.shape[1], hw), functools.partial(
            lambda i, h, m, o: (i, 0, o + h), o=k_blk0)))
        args.append(k_arr)
    for _, _, v_arr, v_blk0 in kv_segs:
        specs.append(pl.BlockSpec((1, v_arr.shape[1], hw), functools.partial(
            lambda i, h, m, o: (i, 0, o + h), o=v_blk0)))
        args.append(v_arr)
    return pl.pallas_call(
        functools.partial(_attn_body, nseg=nseg, lam_init=lam_init),
        grid=(b, DA_HEADS, lq // tq),
        in_specs=specs,
        out_specs=pl.BlockSpec((1, tq, hw), lambda i, h, m: (i, m, h)),
        out_shape=jax.ShapeDtypeStruct((b, lq, DA_WIDTH), BF16),
        compiler_params=_cparams("parallel", "parallel", "arbitrary"),
        name="diff_attention",
    )(*args)


def _merge_body(s_ref, h_ref, a_ref, g0_ref, g1_ref, g2_ref, wb_ref, wo_ref, x_ref, m_ref, ng_ref, o_ref):
    mixed = None
    for br, gl, i in ((s_ref, g0_ref, 0), (h_ref, g1_ref, 1), (a_ref, g2_ref, 2)):
        t = jax.nn.sigmoid(gl[0].astype(F32)) * jnp.dot(br[0], wb_ref[i], preferred_element_type=F32)
        mixed = t if mixed is None else mixed + t
    out = jnp.dot(mixed.astype(BF16), wo_ref[...], preferred_element_type=F32)
    o_ref[0] = x_ref[0] + m_ref[0] * _rms(out, ng_ref[...])


def _merge_residual(ssd, hy, da, big, w_branch, w_out, x, mod_gate, norm_g):
    b, l, d = x.shape
    tm = min(l, 512)
    row = lambda col: pl.BlockSpec((1, tm, d), lambda i, m: (i, m, col))
    return pl.pallas_call(
        _merge_body,
        grid=(b, l // tm),
        in_specs=[row(0), row(0), row(0), row(BLK_GATE), row(BLK_GATE + 1), row(BLK_GATE + 2),
                  pl.BlockSpec((3, d, d), lambda i, m: (0, 0, 0)),
                  pl.BlockSpec((d, d), lambda i, m: (0, 0)),
                  row(0),
                  pl.BlockSpec((1, 1, d), lambda i, m: (i, 0, 0)),
                  pl.BlockSpec((1, d), lambda i, m: (0, 0))],
        out_specs=row(0),
        out_shape=jax.ShapeDtypeStruct((b, l, d), F32),
        compiler_params=_cparams("parallel", "parallel"),
        name="merge_residual",
    )(ssd, hy, da, big, big, big, w_branch, w_out, x, mod_gate.reshape(b, 1, d), norm_g.reshape(1, d))


def _ffn_down_body(gt_ref, p_ref, n_ref, up_ref, cw_ref, cb_ref, wd_ref, x_ref, m_ref, ng_ref, o_ref):
    m, nm = pl.program_id(1), pl.num_programs(1)
    gate = _conv_rows(gt_ref[0].astype(F32), p_ref[0].astype(F32), n_ref[0].astype(F32),
                      cw_ref[...], cb_ref[...], m, nm)
    act = (_silu(gate) * up_ref[0].astype(F32)).astype(BF16)
    out = jnp.dot(act, wd_ref[...], preferred_element_type=F32)
    o_ref[0] = x_ref[0] + m_ref[0] * _rms(out, ng_ref[...])


def _ffn_down_residual(up, conv_w, conv_b, w_down, x, mod_gate, norm_g):
    b, l, d = x.shape
    f = D_FF
    tm = min(l, 512)
    prev, nxt = _halo_specs(tm, l, f, 0)
    return pl.pallas_call(
        _ffn_down_body,
        grid=(b, l // tm),
        in_specs=[pl.BlockSpec((1, tm, f), lambda i, m: (i, m, 0)), prev, nxt,
                  pl.BlockSpec((1, tm, f), lambda i, m: (i, m, 1)),
                  pl.BlockSpec((FFN_CONV, f), lambda i, m: (0, 0)),
                  pl.BlockSpec((1, f), lambda i, m: (0, 0)),
                  pl.BlockSpec((f, d), lambda i, m: (0, 0)),
                  pl.BlockSpec((1, tm, d), lambda i, m: (i, m, 0)),
                  pl.BlockSpec((1, 1, d), lambda i, m: (i, 0, 0)),
                  pl.BlockSpec((1, d), lambda i, m: (0, 0))],
        out_specs=pl.BlockSpec((1, tm, d), lambda i, m: (i, m, 0)),
        out_shape=jax.ShapeDtypeStruct((b, l, d), F32),
        compiler_params=_cparams("parallel", "parallel"),
        name="ffn_down_residual",
    )(up, up, up, up, conv_w.astype(F32), conv_b.astype(F32).reshape(1, f), w_down, x,
      mod_gate.reshape(b, 1, d), norm_g.reshape(1, d))


def _project(x, g, shift, scale, w_big, w_xbc, w_dt):
    big = _norm_mod_matmul(x, g, shift, scale, w_big, 1024, BF16, "proj_big")
    xbc = _norm_mod_matmul(x, g, shift, scale, w_xbc, 512, BF16, "proj_xbc")
    dt = _norm_mod_matmul(x, g, shift, scale, w_dt, LANES, F32, "proj_dt")
    return big, xbc, dt


def _lane_row(v):
    flat = v.astype(F32).reshape(1, 2 * SSD_HEADS)
    return jnp.pad(flat, ((0, 0), (0, LANES - 2 * SSD_HEADS)))


def kernel(x, c, ctx, c_ctx, w_ada, b_ada, norm_g, w_in, ssd_conv_w, ssd_conv_b, ssd_a_log, ssd_dt_bias, ssd_d, ssd_norm, hy_conv_w, hy_conv_b, hy_w1, hy_b1, hy_f1, hy_w2, hy_b2, hy_f2, hy_w3, hy_bias, da_lambda, da_norm, w_branch, w_out, ffn_w_up, ffn_conv_w, ffn_conv_b, ffn_w_down):
    b, seq, d = x.shape
    depth = w_ada.shape[0]
    cos_t, sin_t = _rope_tables(seq)
    mod = _ada_mod(c, c_ctx, w_ada, b_ada)
    x_l, x_c = x, ctx
    hshape = (b, SSD_GROUPS, SSD_STATE, SSD_GROUP_W)
    for i in range(depth):
        ctx_out = i < depth - 1
        ml = [mod[i, :b, k * d:(k + 1) * d] for k in range(6)]
        mc = [jnp.broadcast_to(mod[i, b:b + 1, k * d:(k + 1) * d], (b, d)) for k in range(6)]
        ng = norm_g[i]
        wi = w_in[i]
        w_big = jnp.concatenate([wi[:, :SSD_INNER], wi[:, OFF_HY:]], axis=1).astype(BF16)
        w_xbc = wi[:, SSD_INNER:SSD_INNER + SSD_CONV_CH].astype(BF16)
        w_dt = jnp.pad(wi[:, SSD_INNER + SSD_CONV_CH:IN_SSD], ((0, 0), (0, LANES - 2 * SSD_HEADS))).astype(BF16)
        big_c, xbc_c, dt_c = _project(x_c, ng[0], mc[0], mc[1], w_big, w_xbc, w_dt)
        big_l, xbc_l, dt_l = _project(x_l, ng[0], ml[0], ml[1], w_big, w_xbc, w_dt)

        conv_w = ssd_conv_w[i].astype(F32)
        dtb, alog = _lane_row(ssd_dt_bias[i]), _lane_row(ssd_a_log[i])
        dskip_x = jnp.repeat(ssd_d[i].astype(F32), SSD_HEAD_DIM).reshape(1, SSD_INNER)
        h0 = jnp.zeros(hshape, F32)
        ssd_c, hf, hb = _ssd_branch(_ssd_prep(xbc_c, conv_w, ssd_conv_b[i].astype(F32)), dt_c, big_c,
                                    dtb, alog, dskip_x, ssd_norm[i], h0, h0)
        ssd_l, _, _ = _ssd_branch(_ssd_prep(xbc_l, conv_w, ssd_conv_b[i].astype(F32)), dt_l, big_l,
                                  dtb, alog, dskip_x, ssd_norm[i], hf, hb)

        filt = (hy_w1[i], hy_b1[i], hy_f1[i], hy_w2[i], hy_b2[i], hy_f2[i], hy_w3[i])
        hy_l = _hyena_branch(big_l, hy_conv_w[i], hy_conv_b[i], hy_bias[i].astype(F32), filt, True)

        lam_init = 0.8 - 0.6 * math.exp(-0.3 * i)
        qk_l = _rope_qk(big_l, cos_t, sin_t)
        hpb = D_MODEL // (2 * DA_HEAD_DIM)
        segs = [(big_c, BLK_K * hpb, big_c, BLK_V * hpb), (qk_l, hpb, big_l, BLK_V * hpb)]
        da_l = _diff_attention(qk_l, 0, segs, da_lambda[i], da_norm[i], lam_init)

        wb = w_branch[i].astype(BF16)
        wo = w_out[i].astype(BF16)
        w_up = ffn_w_up[i].astype(BF16)
        w_dn = ffn_w_down[i].astype(BF16)
        x_l = _merge_residual(ssd_l, hy_l, da_l, big_l, wb, wo, x_l, ml[2], ng[1])
        up_l = _norm_mod_matmul(x_l, ng[2], ml[3], ml[4], w_up, 1408, BF16, "ffn_up")
        x_l = _ffn_down_residual(up_l, ffn_conv_w[i], ffn_conv_b[i], w_dn, x_l, ml[5], ng[3])
        if ctx_out:
            hy_c = _hyena_branch(big_c, hy_conv_w[i], hy_conv_b[i], hy_bias[i].astype(F32), filt, False)
            da_c = _diff_attention(big_c, BLK_Q * hpb, [(big_c, BLK_K * hpb, big_c, BLK_V * hpb)],
                                   da_lambda[i], da_norm[i], lam_init)
            x_c = _merge_residual(ssd_c, hy_c, da_c, big_c, wb, wo, x_c, mc[2], ng[1])
            up_c = _norm_mod_matmul(x_c, ng[2], mc[3], mc[4], w_up, 1408, BF16, "ffn_up")
            x_c = _ffn_down_residual(up_c, ffn_conv_w[i], ffn_conv_b[i], w_dn, x_c, mc[5], ng[3])
    return x_l
```

```python
import functools
import math

import numpy as np
import jax
import jax.numpy as jnp
from jax import lax
from jax.experimental import pallas as pl
from jax.experimental.pallas import tpu as pltpu

F32 = jnp.float32
BF16 = jnp.bfloat16
HIGHEST = lax.Precision.HIGHEST

D_MODEL = 1024
GRID_W = 64
EPS = 1e-6

SSD_INNER = D_MODEL
SSD_HEAD_DIM = 64
SSD_HEADS = SSD_INNER // SSD_HEAD_DIM
SSD_GROUPS = 2
SSD_STATE = 128
SSD_CONV = 5
SSD_CHUNK = 128
SSD_CONV_CH = SSD_INNER + 2 * SSD_GROUPS * SSD_STATE
SSD_GROUP_W = SSD_INNER // SSD_GROUPS

HY_WIDTH = D_MODEL
HY_ORDER = 2
HY_SHORT = 3
HY_BANDS = 16
HY_EMB = 1 + 2 * HY_BANDS
HY_FF = 64
HY_TARGET = 1e-2
HY_FAST = 0.3
HY_SLOW = 1.5
HY_N2 = 64
HY_CB = 128
HY_CB1 = 512
HY_PAD = 8
HY_UNROLL = 8

DA_HEAD_DIM = 64
DA_HEADS = D_MODEL // (2 * DA_HEAD_DIM)
DA_WIDTH = DA_HEADS * 2 * DA_HEAD_DIM
ROPE_BASE = 10000.0
ATT_TQ = 256
ATT_KC = 512

D_FF = ((8 * D_MODEL // 3 + 127) // 128) * 128
FFN_CONV = 3

IN_SSD = SSD_INNER + SSD_CONV_CH + 2 * SSD_HEADS
IN_HY = (HY_ORDER + 1) * HY_WIDTH
IN_DA = 3 * DA_WIDTH
OFF_HY = IN_SSD
OFF_DA = OFF_HY + IN_HY
OFF_GATE = OFF_DA + IN_DA

BLK_Z, BLK_HY, BLK_Q, BLK_K, BLK_V, BLK_GATE = 0, 1, 4, 5, 6, 7
BIG_COLS = 10 * D_MODEL

VMEM_LIMIT = 56 * 1024 * 1024
LANES = 128
HALO = 16


def _cparams(*sem):
    return pltpu.CompilerParams(dimension_semantics=sem, vmem_limit_bytes=VMEM_LIMIT)


def _silu(x):
    return x * jax.nn.sigmoid(x)


def _rms(x, g):
    return x * lax.rsqrt(jnp.mean(x * x, axis=-1, keepdims=True) + EPS) * g


def _round_up(a, m):
    return (a + m - 1) // m * m


def _ada_body(s_ref, w_ref, b_ref, o_ref):
    s = _silu(s_ref[...])
    o_ref[0] = jnp.dot(s, w_ref[0], preferred_element_type=F32, precision=HIGHEST) + b_ref[0]


def _ada_mod(c, c_ctx, w_ada, b_ada):
    depth, d, n6 = w_ada.shape
    b = c.shape[0]
    rows = _round_up(b + 1, 8)
    s = jnp.zeros((rows, d), F32).at[:b].set(c).at[b].set(c_ctx)
    tn = 1536
    return pl.pallas_call(
        _ada_body,
        grid=(depth, n6 // tn),
        in_specs=[pl.BlockSpec((rows, d), lambda l, j: (0, 0)),
                  pl.BlockSpec((1, d, tn), lambda l, j: (l, 0, j)),
                  pl.BlockSpec((1, 1, tn), lambda l, j: (l, 0, j))],
        out_specs=pl.BlockSpec((1, rows, tn), lambda l, j: (l, 0, j)),
        out_shape=jax.ShapeDtypeStruct((depth, rows, n6), F32),
        compiler_params=_cparams("parallel", "parallel"),
        name="ada_mod",
    )(s, w_ada, b_ada.reshape(depth, 1, n6))


def _nmm_body(x_ref, g_ref, sh_ref, sc_ref, w_ref, o_ref, h_scr):
    @pl.when(pl.program_id(2) == 0)
    def _():
        h = _rms(x_ref[0], g_ref[...]) * (1.0 + sc_ref[0]) + sh_ref[0]
        h_scr[...] = h.astype(BF16)

    o_ref[0] = jnp.dot(h_scr[...], w_ref[...], preferred_element_type=F32).astype(o_ref.dtype)


def _norm_mod_matmul(x, g, shift, scale, w, tn, out_dtype, name):
    b, l, d = x.shape
    n = w.shape[1]
    tm = min(l, 1024)
    return pl.pallas_call(
        _nmm_body,
        grid=(b, l // tm, n // tn),
        in_specs=[pl.BlockSpec((1, tm, d), lambda i, m, j: (i, m, 0)),
                  pl.BlockSpec((1, d), lambda i, m, j: (0, 0)),
                  pl.BlockSpec((1, 1, d), lambda i, m, j: (i, 0, 0)),
                  pl.BlockSpec((1, 1, d), lambda i, m, j: (i, 0, 0)),
                  pl.BlockSpec((d, tn), lambda i, m, j: (0, j))],
        out_specs=pl.BlockSpec((1, tm, tn), lambda i, m, j: (i, m, j)),
        out_shape=jax.ShapeDtypeStruct((b, l, n), out_dtype),
        scratch_shapes=[pltpu.VMEM((tm, d), BF16)],
        compiler_params=_cparams("parallel", "parallel", "arbitrary"),
        name=name,
    )(x, g.reshape(1, d), shift.reshape(b, 1, d), scale.reshape(b, 1, d), w)


def _halo_specs(tl, l, c, col):
    per = tl // HALO
    last = l // HALO - 1
    prev = pl.BlockSpec((1, HALO, c), lambda i, m: (i, jnp.maximum(m * per - 1, 0), col))
    nxt = pl.BlockSpec((1, HALO, c), lambda i, m: (i, jnp.minimum((m + 1) * per, last), col))
    return prev, nxt


def _conv_rows(x, prev, nxt, w, bias, m, nm):
    tl = x.shape[0]
    k = w.shape[0]
    pad = k // 2
    prev = prev * (m > 0).astype(F32)
    nxt = nxt * (m < nm - 1).astype(F32)
    ext = jnp.concatenate([prev, x, nxt], axis=0)
    rows = tl + 2 * HALO
    y = bias
    for j in range(k):
        shifted = ext if j == pad else pltpu.roll(ext, (pad - j) % rows, 0)
        y = y + shifted[HALO:HALO + tl] * w[j:j + 1]
    return y


def _ssd_prep_body(x_ref, p_ref, n_ref, w_ref, b_ref, o_ref):
    m, nm = pl.program_id(1), pl.num_programs(1)
    y = _conv_rows(x_ref[0].astype(F32), p_ref[0].astype(F32), n_ref[0].astype(F32),
                   w_ref[...], b_ref[...], m, nm)
    o_ref[0] = _silu(y).astype(o_ref.dtype)


def _ssd_prep(xbc_raw, conv_w, conv_b):
    b, l, c = xbc_raw.shape
    tl = min(l, 512)
    prev, nxt = _halo_specs(tl, l, c, 0)
    return pl.pallas_call(
        _ssd_prep_body,
        grid=(b, l // tl),
        in_specs=[pl.BlockSpec((1, tl, c), lambda i, m: (i, m, 0)), prev, nxt,
                  pl.BlockSpec((SSD_CONV, c), lambda i, m: (0, 0)),
                  pl.BlockSpec((1, c), lambda i, m: (0, 0))],
        out_specs=pl.BlockSpec((1, tl, c), lambda i, m: (i, m, 0)),
        out_shape=jax.ShapeDtypeStruct((b, l, c), BF16),
        compiler_params=_cparams("parallel", "parallel"),
        name="ssd_prep",
    )(xbc_raw, xbc_raw, xbc_raw, conv_w, conv_b.reshape(1, c))


def _softplus(x):
    return jnp.maximum(x, 0.0) + jnp.log1p(jnp.exp(-jnp.abs(x)))


def _ssd_chunk(xbc, dt_raw, dtb, alog, h_scr, reverse, col0):
    cs = xbc.shape[0]
    gw = SSD_GROUPS * SSD_STATE
    x = xbc[:, :SSD_INNER].astype(F32)
    dt = _softplus(dt_raw + dtb)
    da = dt * (-jnp.exp(alog))
    row = lax.broadcasted_iota(jnp.int32, (cs, cs), 0)
    col = lax.broadcasted_iota(jnp.int32, (cs, cs), 1)
    keep = (row <= col) if reverse else (row >= col)
    acum = jnp.dot(keep.astype(F32), da, preferred_element_type=F32, precision=HIGHEST)
    acum_t = acum.T
    head_of_lane = col0 + lax.broadcasted_iota(jnp.int32, (LANES, SSD_INNER), 1) // SSD_HEAD_DIM
    expand = (lax.broadcasted_iota(jnp.int32, (LANES, SSD_INNER), 0) == head_of_lane).astype(F32)
    acum_x = jnp.dot(acum, expand, preferred_element_type=F32, precision=HIGHEST)
    dt_x = jnp.dot(dt, expand, preferred_element_type=F32, precision=HIGHEST)
    end = 0 if reverse else cs - 1
    aend_x = acum_x[end:end + 1]
    xdt = x * dt_x
    xw = (xdt * jnp.exp(aend_x - acum_x)).astype(BF16)
    into = jnp.exp(acum_x)
    dec = jnp.exp(aend_x)
    xdt_b = xdt.astype(BF16)
    lane = lax.broadcasted_iota(jnp.int32, (cs, LANES), 1)
    parts = []
    for g in range(SSD_GROUPS):
        bm = xbc[:, SSD_INNER + g * SSD_STATE:SSD_INNER + (g + 1) * SSD_STATE]
        cm = xbc[:, SSD_INNER + gw + g * SSD_STATE:SSD_INNER + gw + (g + 1) * SSD_STATE]
        cb = lax.dot_general(cm, bm, (((1,), (1,)), ((), ())), preferred_element_type=F32)
        sl = slice(g * SSD_GROUP_W, (g + 1) * SSD_GROUP_W)
        h_t = h_scr[g]
        y_off = jnp.dot(cm, h_t.astype(BF16), preferred_element_type=F32) * into[:, sl]
        bm_t = bm.astype(F32).T.astype(BF16)
        h_scr[g] = h_t * dec[:, sl] + jnp.dot(bm_t, xw[:, sl], preferred_element_type=F32)
        for p in range(SSD_GROUP_W // LANES):
            lo = g * SSD_GROUP_W + p * LANES
            xp = xdt_b[:, lo:lo + LANES]
            ys = []
            for q in range(LANES // SSD_HEAD_DIM):
                hc = col0 + lo // SSD_HEAD_DIM + q
                seg = acum[:, hc:hc + 1] - acum_t[hc:hc + 1, :]
                dmat = (cb * jnp.exp(jnp.where(keep, seg, -1e30))).astype(BF16)
                ys.append(jnp.dot(dmat, xp, preferred_element_type=F32))
            y_diag = jnp.where(lane < SSD_HEAD_DIM, ys[0], ys[1])
            parts.append(y_diag + y_off[:, p * LANES:(p + 1) * LANES])
    return jnp.concatenate(parts, axis=1), x


def _ssd_fwd_body(xbc_ref, dt_ref, dtb_ref, alog_ref, h0_ref, y_ref, hl_ref, h_scr):
    c = pl.program_id(1)

    @pl.when(c == 0)
    def _():
        h_scr[...] = h0_ref[0]

    y, _ = _ssd_chunk(xbc_ref[0], dt_ref[0], dtb_ref[...], alog_ref[...], h_scr, False, 0)
    y_ref[0] = y

    @pl.when(c == pl.num_programs(1) - 1)
    def _():
        hl_ref[0] = h_scr[...]


def _ssd_bwd_body(xbc_ref, dt_ref, dtb_ref, alog_ref, h0_ref, yf_ref, z_ref, dsk_ref, ng_ref,
                  o_ref, hl_ref, h_scr):
    c = pl.program_id(1)

    @pl.when(c == 0)
    def _():
        h_scr[...] = h0_ref[0]

    yb, x = _ssd_chunk(xbc_ref[0], dt_ref[0], dtb_ref[...], alog_ref[...], h_scr, True, SSD_HEADS)
    y = yf_ref[0] + yb + x * dsk_ref[...]
    o_ref[0] = _rms(y * _silu(z_ref[0].astype(F32)), ng_ref[...]).astype(o_ref.dtype)

    @pl.when(c == pl.num_programs(1) - 1)
    def _():
        hl_ref[0] = h_scr[...]


def _ssd_branch(xbc_act, dt_raw, big, dtb, alog, dskip_x, norm_g, h0f, h0b):
    b, l, _ = xbc_act.shape
    cs = SSD_CHUNK
    nc = l // cs
    hshape = (SSD_GROUPS, SSD_STATE, SSD_GROUP_W)
    state_spec = pl.BlockSpec((1,) + hshape, lambda i, c: (i, 0, 0, 0))
    vec = lambda w: pl.BlockSpec((1, w), lambda i, c: (0, 0))
    state_shape = jax.ShapeDtypeStruct((b,) + hshape, F32)
    yf, hf = pl.pallas_call(
        _ssd_fwd_body,
        grid=(b, nc),
        in_specs=[pl.BlockSpec((1, cs, SSD_CONV_CH), lambda i, c: (i, c, 0)),
                  pl.BlockSpec((1, cs, LANES), lambda i, c: (i, c, 0)),
                  vec(LANES), vec(LANES), state_spec],
        out_specs=[pl.BlockSpec((1, cs, SSD_INNER), lambda i, c: (i, c, 0)), state_spec],
        out_shape=[jax.ShapeDtypeStruct((b, l, SSD_INNER), F32), state_shape],
        scratch_shapes=[pltpu.VMEM(hshape, F32)],
        compiler_params=_cparams("parallel", "arbitrary"),
        name="ssd_scan_fwd",
    )(xbc_act, dt_raw, dtb, alog, h0f)
    rev = lambda i, c: (i, nc - 1 - c, 0)
    out, hb = pl.pallas_call(
        _ssd_bwd_body,
        grid=(b, nc),
        in_specs=[pl.BlockSpec((1, cs, SSD_CONV_CH), rev),
                  pl.BlockSpec((1, cs, LANES), rev),
                  vec(LANES), vec(LANES), state_spec,
                  pl.BlockSpec((1, cs, SSD_INNER), rev),
                  pl.BlockSpec((1, cs, SSD_INNER), lambda i, c: (i, nc - 1 - c, BLK_Z)),
                  vec(SSD_INNER), vec(SSD_INNER)],
        out_specs=[pl.BlockSpec((1, cs, SSD_INNER), rev), state_spec],
        out_shape=[jax.ShapeDtypeStruct((b, l, SSD_INNER), BF16), state_shape],
        scratch_shapes=[pltpu.VMEM(hshape, F32)],
        compiler_params=_cparams("parallel", "arbitrary"),
        name="ssd_scan_bwd",
    )(xbc_act, dt_raw, dtb, alog, h0b, yf, big, dskip_x, norm_g.reshape(1, SSD_INNER))
    return out, hf, hb


def _hy_filter_body(z_ref, w1_ref, b1_ref, f1_ref, w2_ref, b2_ref, f2_ref, w3_ref, dl_ref, o_ref, *, n, tr):
    z = z_ref[...]
    h = jnp.sin(f1_ref[...] * (jnp.dot(z, w1_ref[...], preferred_element_type=F32, precision=HIGHEST)
                               + b1_ref[...]))
    h = jnp.sin(f2_ref[...] * (jnp.dot(h, w2_ref[...], preferred_element_type=F32, precision=HIGHEST)
                               + b2_ref[...]))
    filt = jnp.dot(h, w3_ref[...], preferred_element_type=F32, precision=HIGHEST)
    filt = filt * jnp.exp(-z[:, 0:1] * dl_ref[...])
    rows = pl.program_id(0) * tr + lax.broadcasted_iota(jnp.int32, (tr, 1), 0)
    o_ref[...] = jnp.where(rows == n, 0.0, filt)


def _hy_features(n):
    t = jnp.linspace(0.0, 1.0, n, dtype=F32)[:, None]
    w = (2.0 * math.pi / n) * jnp.arange(n, dtype=F32)[:, None]
    bands = jnp.linspace(1e-4, HY_BANDS - 1, HY_BANDS, dtype=F32)
    z = jnp.concatenate([t, jnp.cos(bands * w), -jnp.sin(bands * w)], axis=-1)
    return jnp.pad(z, ((0, 0), (0, LANES - HY_EMB)))


def _hy_filter(n, w1, b1, f1, w2, b2, f2, w3):
    z = _hy_features(n)
    zk = jnp.concatenate([z, z[:1], jnp.flip(z[1:], axis=0)], axis=0)
    padw = lambda a, r, c: jnp.pad(a.astype(F32), ((0, r - a.shape[0]), (0, c - a.shape[1])))
    w1p = padw(w1, LANES, LANES)
    w2p = padw(w2, LANES, LANES)
    w3p = padw(w3, LANES, w3.shape[1])
    rowp = lambda a: padw(a.reshape(1, -1), 1, LANES)
    deltas = jnp.abs(jnp.linspace(math.log(HY_TARGET) / HY_SLOW, math.log(HY_TARGET) / HY_FAST,
                                  HY_WIDTH, dtype=F32))
    dl = jnp.tile(deltas, HY_ORDER).reshape(1, HY_ORDER * HY_WIDTH)
    tr = min(n, 512)
    wide = HY_ORDER * HY_WIDTH
    full = lambda r, c: pl.BlockSpec((r, c), lambda i: (0, 0))
    return pl.pallas_call(
        functools.partial(_hy_filter_body, n=n, tr=tr),
        grid=(2 * n // tr,),
        in_specs=[pl.BlockSpec((tr, LANES), lambda i: (i, 0)),
                  full(LANES, LANES), full(1, LANES), full(1, LANES),
                  full(LANES, LANES), full(1, LANES), full(1, LANES),
                  pl.BlockSpec((LANES, wide), lambda i: (0, (i * tr) // n)),
                  full(1, wide)],
        out_specs=pl.BlockSpec((tr, wide), lambda i: (i, 0)),
        out_shape=jax.ShapeDtypeStruct((2 * n, wide), F32),
        compiler_params=_cparams("parallel"),
        name="hy_filter",
    )(zk, w1p, rowp(b1), rowp(f1), w2p, rowp(b2), rowp(f2), w3p, dl)


def _two_stage_tables(n, filt_rows):
    big_n = 2 * n
    n2 = HY_N2
    n1 = big_n // n2
    k1n = n1 // 2 + 1
    p = _round_up(k1n, 8)
    s2 = np.arange(n2)
    k1 = np.arange(k1n)

    def first(s_rows):
        s1 = np.arange(s_rows)
        ang = 2.0 * np.pi * ((k1[None, :, None] * (n2 * s1[None, None, :] + s2[:, None, None])) % big_n) / big_n
        out = np.zeros((n2, 2 * p, s_rows), np.float32)
        out[:, :k1n] = np.cos(ang)
        out[:, p:p + k1n] = -np.sin(ang)
        return out

    ang2 = 2.0 * np.pi * ((s2[:, None] * s2[None, :]) % n2) / n2
    c2, sn2 = np.cos(ang2), np.sin(ang2)
    second = np.block([[c2, sn2], [-sn2, c2]]).astype(np.float32)
    second_inv = np.block([[c2, -sn2], [sn2, c2]]).astype(np.float32)
    s1o = np.arange(n // n2)
    wgt = np.where((k1 == 0) | (k1 == n1 // 2), 1.0, 2.0) / big_n
    ang = 2.0 * np.pi * ((k1[None, None, :] * (n2 * s1o[None, :, None] + s2[:, None, None])) % big_n) / big_n
    last = np.zeros((n2, n // n2, 2 * p), np.float32)
    last[:, :, :k1n] = wgt * np.cos(ang)
    last[:, :, p:p + k1n] = -wgt * np.sin(ang)
    cast = lambda a: jnp.asarray(a).astype(BF16)
    return dict(k1n=k1n, p=p, first=cast(first(n // n2)), first_filt=cast(first(filt_rows // n2)),
                second=cast(second), second_inv=cast(second_inv), last=cast(last))


def _one_stage_tables(n):
    big_n = 2 * n
    kn = n + 1
    p = _round_up(kn, 8)
    k = np.arange(kn)

    def fwd(rows):
        s = np.arange(rows)
        ang = 2.0 * np.pi * ((k[:, None] * s[None, :]) % big_n) / big_n
        out = np.zeros((2 * p, rows), np.float32)
        out[:kn] = np.cos(ang)
        out[p:p + kn] = -np.sin(ang)
        return out

    s = np.arange(n)
    wgt = np.where((k == 0) | (k == n), 1.0, 2.0) / big_n
    ang = 2.0 * np.pi * ((s[:, None] * k[None, :]) % big_n) / big_n
    inv = np.zeros((n, 2 * p), np.float32)
    inv[:, :kn] = wgt * np.cos(ang)
    inv[:, p:p + kn] = -wgt * np.sin(ang)
    cast = lambda a: jnp.asarray(a).astype(BF16)
    return dict(p=p, fwd=cast(fwd(n)), fwd_filt=cast(fwd(big_n)), inv=cast(inv))


def _spec2_body(k_ref, f1_ref, f2_ref, o_ref, a_scr, *, k1n, p, s_rows):
    n2 = HY_N2

    def stage1(s2, carry):
        rows = k_ref[pl.ds(s2, s_rows, stride=n2), :]
        a = jnp.dot(f1_ref[s2], rows.astype(BF16), preferred_element_type=F32)
        a_scr[pl.ds(pl.multiple_of(s2 * 2 * p, 8), 2 * p), :] = a
        return carry

    lax.fori_loop(0, n2, stage1, 0, unroll=HY_UNROLL)

    def stage2(k1, carry):
        re = a_scr[pl.ds(k1, n2, stride=2 * p), :]
        im = a_scr[pl.ds(p + k1, n2, stride=2 * p), :]
        a = jnp.concatenate([re, im], axis=0).astype(BF16)
        o_ref[k1] = jnp.dot(f2_ref[...], a, preferred_element_type=F32)
        return carry

    lax.fori_loop(0, k1n, stage2, 0, unroll=HY_UNROLL)


def _hy_spectrum2(kfilt, tabs):
    rows, wide = kfilt.shape
    k1n, p = tabs["k1n"], tabs["p"]
    s_rows = rows // HY_N2
    cb = HY_CB
    return pl.pallas_call(
        functools.partial(_spec2_body, k1n=k1n, p=p, s_rows=s_rows),
        grid=(wide // cb,),
        in_specs=[pl.BlockSpec((rows, cb), lambda j: (0, j)),
                  pl.BlockSpec((HY_N2, 2 * p, s_rows), lambda j: (0, 0, 0)),
                  pl.BlockSpec((2 * HY_N2, 2 * HY_N2), lambda j: (0, 0))],
        out_specs=pl.BlockSpec((k1n, 2 * HY_N2, cb), lambda j: (0, 0, j)),
        out_shape=jax.ShapeDtypeStruct((k1n, 2 * HY_N2, wide), F32),
        scratch_shapes=[pltpu.VMEM((HY_N2 * 2 * p, cb), F32)],
        compiler_params=_cparams("parallel"),
        name="hy_spectrum2",
    )(kfilt, tabs["first_filt"], tabs["second"])


def _spec1_body(k_ref, f_ref, o_ref):
    o_ref[...] = jnp.dot(f_ref[...], k_ref[...].astype(BF16), preferred_element_type=F32)


def _hy_spectrum1(kfilt, tabs):
    rows, wide = kfilt.shape
    p = tabs["p"]
    cb = HY_CB1
    return pl.pallas_call(
        _spec1_body,
        grid=(wide // cb,),
        in_specs=[pl.BlockSpec((rows, cb), lambda j: (0, j)),
                  pl.BlockSpec((2 * p, rows), lambda j: (0, 0))],
        out_specs=pl.BlockSpec((2 * p, cb), lambda j: (0, j)),
        out_shape=jax.ShapeDtypeStruct((2 * p, wide), F32),
        compiler_params=_cparams("parallel"),
        name="hy_spectrum1",
    )(kfilt, tabs["fwd_filt"])


def _short_conv_seq(src_ref, w_ref, b_ref, pad_scr, dst_scr, n, rb):
    for r in range(0, n, rb):
        pad_scr[pl.ds(HY_PAD + r, rb), :] = src_ref[0, pl.ds(r, rb), :].astype(F32)
    for r in range(0, n, rb):
        acc = b_ref[...]
        for j in range(HY_SHORT):
            acc = acc + pad_scr[pl.ds(HY_PAD - 1 + j + r, rb), :] * w_ref[j:j + 1, :]
        dst_scr[pl.ds(r, rb), :] = acc


def _zero_pad_rows(pad_scr, n):
    zeros = jnp.zeros((HY_PAD, pad_scr.shape[1]), F32)
    pad_scr[pl.ds(0, HY_PAD), :] = zeros
    pad_scr[pl.ds(HY_PAD + n, HY_PAD), :] = zeros


def _load_u(u_ref, cwu_ref, cbu_ref, pad_scr, u_scr, n, rb, conv_u):
    if conv_u:
        _short_conv_seq(u_ref, cwu_ref, cbu_ref, pad_scr, u_scr, n, rb)
    else:
        for r in range(0, n, rb):
            u_scr[pl.ds(r, rb), :] = u_ref[0, pl.ds(r, rb), :].astype(F32)


def _gate_out(g_ref, cwg_ref, cbg_ref, bias_ref, pad_scr, u_scr, y_scr, o_ref, n, rb):
    for r in range(0, n, rb):
        pad_scr[pl.ds(HY_PAD + r, rb), :] = g_ref[0, pl.ds(r, rb), :].astype(F32)
    for r in range(0, n, rb):
        gate = cbg_ref[...]
        for j in range(HY_SHORT):
            gate = gate + pad_scr[pl.ds(HY_PAD - 1 + j + r, rb), :] * cwg_ref[j:j + 1, :]
        y = y_scr[pl.ds(r, rb), :] + u_scr[pl.ds(r, rb), :] * bias_ref[...]
        o_ref[0, pl.ds(r, rb), :] = (gate * y).astype(o_ref.dtype)


def _conv2_body(u_ref, g_ref, cwu_ref, cbu_ref, cwg_ref, cbg_ref, bias_ref, kf_ref,
                f1_ref, f2_ref, i1_ref, l_ref, o_ref, pad_scr, u_scr, y_scr, a_scr, b_scr,
                *, n, k1n, p, conv_u):
    n2 = HY_N2
    s_rows = n // n2
    rb = min(n, 512)
    _zero_pad_rows(pad_scr, n)
    _load_u(u_ref, cwu_ref, cbu_ref, pad_scr, u_scr, n, rb, conv_u)

    def stage1(s2, carry):
        rows = u_scr[pl.ds(s2, s_rows, stride=n2), :]
        a = jnp.dot(f1_ref[s2], rows.astype(BF16), preferred_element_type=F32)
        a_scr[pl.ds(pl.multiple_of(s2 * 2 * p, 8), 2 * p), :] = a
        return carry

    lax.fori_loop(0, n2, stage1, 0, unroll=HY_UNROLL)

    if p > k1n:
        b_scr[pl.ds(k1n * 2 * n2, (p - k1n) * 2 * n2), :] = jnp.zeros(((p - k1n) * 2 * n2, b_scr.shape[1]), F32)

    def stage2(k1, carry):
        re = a_scr[pl.ds(k1, n2, stride=2 * p), :]
        im = a_scr[pl.ds(p + k1, n2, stride=2 * p), :]
        x = jnp.dot(f2_ref[...], jnp.concatenate([re, im], axis=0).astype(BF16), preferred_element_type=F32)
        kf = kf_ref[k1]
        xr, xi, kr, ki = x[:n2], x[n2:], kf[:n2], kf[n2:]
        prod = jnp.concatenate([xr * kr - xi * ki, xr * ki + xi * kr], axis=0).astype(BF16)
        b_scr[pl.ds(pl.multiple_of(k1 * 2 * n2, 8), 2 * n2), :] = jnp.dot(
            i1_ref[...], prod, preferred_element_type=F32)
        return carry

    lax.fori_loop(0, k1n, stage2, 0, unroll=HY_UNROLL)

    def stage3(s2, carry):
        re = b_scr[pl.ds(s2, p, stride=2 * n2), :]
        im = b_scr[pl.ds(n2 + s2, p, stride=2 * n2), :]
        y = jnp.dot(l_ref[s2], jnp.concatenate([re, im], axis=0).astype(BF16), preferred_element_type=F32)
        y_scr[pl.ds(s2, s_rows, stride=n2), :] = y
        return carry

    lax.fori_loop(0, n2, stage3, 0, unroll=HY_UNROLL)
    _gate_out(g_ref, cwg_ref, cbg_ref, bias_ref, pad_scr, u_scr, y_scr, o_ref, n, rb)


def _conv1_body(u_ref, g_ref, cwu_ref, cbu_ref, cwg_ref, cbg_ref, bias_ref, kf_ref,
                f_ref, inv_ref, o_ref, pad_scr, u_scr, y_scr, *, n, p, conv_u):
    rb = n
    _zero_pad_rows(pad_scr, n)
    _load_u(u_ref, cwu_ref, cbu_ref, pad_scr, u_scr, n, rb, conv_u)
    x = jnp.dot(f_ref[...], u_scr[...].astype(BF16), preferred_element_type=F32)
    kf = kf_ref[...]
    xr, xi, kr, ki = x[:p], x[p:], kf[:p], kf[p:]
    prod = jnp.concatenate([xr * kr - xi * ki, xr * ki + xi * kr], axis=0).astype(BF16)
    y_scr[...] = jnp.dot(inv_ref[...], prod, preferred_element_type=F32)
    _gate_out(g_ref, cwg_ref, cbg_ref, bias_ref, pad_scr, u_scr, y_scr, o_ref, n, rb)


def _hy_long_conv(u_arr, u_blk, conv_u, g_arr, g_blk, conv_w, conv_b, bias, kf, order, tabs, two_stage):
    b, n, _ = g_arr.shape
    c = HY_WIDTH
    cb = HY_CB if two_stage else HY_CB1
    per = c // cb
    ublk = u_blk * per
    gblk = g_blk * per
    wu = ((u_blk - BLK_HY) if conv_u else 0) * per
    wg = (g_blk - BLK_HY) * per
    seq = lambda off: pl.BlockSpec((1, n, cb), lambda j, i: (i, 0, off + j))
    wrow = lambda rows, off: pl.BlockSpec((rows, cb), lambda j, i: (0, off + j))
    const = lambda shape: pl.BlockSpec(shape, lambda j, i: (0,) * len(shape))
    common = [seq(ublk), seq(gblk), wrow(HY_SHORT, wu), wrow(1, wu), wrow(HY_SHORT, wg), wrow(1, wg),
              wrow(1, order * per)]
    args = [u_arr, g_arr, conv_w, conv_b, conv_w, conv_b, bias.reshape(1, HY_ORDER * c)]
    scratch = [pltpu.VMEM((n + 2 * HY_PAD, cb), F32), pltpu.VMEM((n, cb), F32), pltpu.VMEM((n, cb), F32)]
    if two_stage:
        k1n, p = tabs["k1n"], tabs["p"]
        n2 = HY_N2
        body = functools.partial(_conv2_body, n=n, k1n=k1n, p=p, conv_u=conv_u)
        specs = common + [pl.BlockSpec((k1n, 2 * n2, cb), lambda j, i: (0, 0, order * per + j)),
                          const((n2, 2 * p, n // n2)), const((2 * n2, 2 * n2)), const((2 * n2, 2 * n2)),
                          const((n2, n // n2, 2 * p))]
        args += [kf, tabs["first"], tabs["second"], tabs["second_inv"], tabs["last"]]
        scratch += [pltpu.VMEM((n2 * 2 * p, cb), F32), pltpu.VMEM((p * 2 * n2, cb), F32)]
        name = "hy_conv2"
    else:
        p = tabs["p"]
        body = functools.partial(_conv1_body, n=n, p=p, conv_u=conv_u)
        specs = common + [pl.BlockSpec((2 * p, cb), lambda j, i: (0, order * per + j)),
                          const((2 * p, n)), const((n, 2 * p))]
        args += [kf, tabs["fwd"], tabs["inv"]]
        name = "hy_conv1"
    return pl.pallas_call(
        body,
        grid=(per, b),
        in_specs=specs,
        out_specs=pl.BlockSpec((1, n, cb), lambda j, i: (i, 0, j)),
        out_shape=jax.ShapeDtypeStruct((b, n, c), BF16),
        scratch_shapes=scratch,
        compiler_params=_cparams("parallel", "parallel"),
        name=name,
    )(*args)


def _hyena_branch(big, conv_w, conv_b, bias, filt_params, two_stage):
    b, n, _ = big.shape
    kfilt = _hy_filter(n, *filt_params)
    if two_stage:
        tabs = _two_stage_tables(n, 2 * n)
        kf = _hy_spectrum2(kfilt, tabs)
    else:
        tabs = _one_stage_tables(n)
        kf = _hy_spectrum1(kfilt, tabs)
    cw = conv_w.astype(F32)
    cbias = conv_b.astype(F32).reshape(1, -1)
    z = _hy_long_conv(big, BLK_HY, True, big, BLK_HY + 1, cw, cbias, bias, kf, 0, tabs, two_stage)
    return _hy_long_conv(z, 0, False, big, BLK_HY + 2, cw, cbias, bias, kf, 1, tabs, two_stage)


def _rope_tables(n):
    rows = n // GRID_W
    row = jnp.repeat(jnp.arange(rows), GRID_W)
    col = jnp.tile(jnp.arange(GRID_W), rows)
    quarter = DA_HEAD_DIM // 4
    inv = ROPE_BASE ** (-jnp.arange(quarter, dtype=F32) / quarter)
    ang = jnp.stack([row, col], axis=-1).astype(F32)[..., None] * inv
    cos, sin = jnp.cos(ang), jnp.sin(ang)
    cos_h = jnp.concatenate([cos, cos], axis=-1).reshape(n, DA_HEAD_DIM)
    sin_h = jnp.concatenate([-sin, sin], axis=-1).reshape(n, DA_HEAD_DIM)
    return jnp.tile(cos_h, (1, 2)), jnp.tile(sin_h, (1, 2))


def _rope_body(x_ref, c_ref, s_ref, o_ref):
    quarter = DA_HEAD_DIM // 4
    cos, sin = c_ref[...], s_ref[...]
    lane = lax.broadcasted_iota(jnp.int32, cos.shape, 1)
    first_half = (lane % (2 * quarter)) < quarter
    for k in range(x_ref.shape[2] // LANES):
        x = x_ref[0, :, k * LANES:(k + 1) * LANES].astype(F32)
        partner = jnp.where(first_half, pltpu.roll(x, LANES - quarter, 1), pltpu.roll(x, quarter, 1))
        o_ref[0, :, k * LANES:(k + 1) * LANES] = (x * cos + partner * sin).astype(o_ref.dtype)


def _rope_qk(big, cos_t, sin_t):
    b, l, _ = big.shape
    tm = min(l, 512)
    return pl.pallas_call(
        _rope_body,
        grid=(b, l // tm, 2),
        in_specs=[pl.BlockSpec((1, tm, DA_WIDTH), lambda i, m, j: (i, m, BLK_Q + j)),
                  pl.BlockSpec((tm, LANES), lambda i, m, j: (m, 0)),
                  pl.BlockSpec((tm, LANES), lambda i, m, j: (m, 0))],
        out_specs=pl.BlockSpec((1, tm, DA_WIDTH), lambda i, m, j: (i, m, j)),
        out_shape=jax.ShapeDtypeStruct((b, l, 2 * DA_WIDTH), BF16),
        compiler_params=_cparams("parallel", "parallel", "parallel"),
        name="rope_qk",
    )(big, cos_t, sin_t)


def _attn_body(*refs, nseg, lam_init):
    lp_ref, g_ref, q_ref = refs[:3]
    k_refs = refs[3:3 + nseg]
    v_refs = refs[3 + nseg:3 + 2 * nseg]
    o_ref = refs[3 + 2 * nseg]
    lp = lp_ref[...]
    lam = (jnp.exp(jnp.sum(lp[0:1] * lp[1:2], axis=1, keepdims=True))
           - jnp.exp(jnp.sum(lp[2:3] * lp[3:4], axis=1, keepdims=True)) + lam_init)
    q = (q_ref[0].astype(F32) * (DA_HEAD_DIM ** -0.5 * math.log2(math.e))).astype(BF16)
    lane = lax.broadcasted_iota(jnp.int32, q.shape, 1)
    zero = jnp.zeros_like(q)
    probs = []
    for mp in range(2):
        qm = jnp.where((lane // DA_HEAD_DIM) == mp, q, zero)
        s = [lax.dot_general(qm, k_ref[0], (((1,), (1,)), ((), ())), preferred_element_type=F32)
             for k_ref in k_refs]
        mx = functools.reduce(jnp.maximum, [jnp.max(t, axis=1, keepdims=True) for t in s])
        e = [jnp.exp2(t - mx) for t in s]
        tot = functools.reduce(jnp.add, [jnp.sum(t, axis=1, keepdims=True) for t in e])
        probs.append((e, 1.0 / tot))
    (e0, r0), (e1, r1) = probs
    r1 = r1 * lam
    o = None
    for t0, t1, v_ref in zip(e0, e1, v_refs):
        a = (t0 * r0 - t1 * r1).astype(BF16)
        part = jnp.dot(a, v_ref[0], preferred_element_type=F32)
        o = part if o is None else o + part
    o_ref[0] = (_rms(o, g_ref[...]) * (1.0 - lam_init)).astype(o_ref.dtype)


def _diff_attention(q_arr, q_blk0, kv_segs, lam_p, norm_g, lam_init):
    b, lq, _ = q_arr.shape
    tq = min(lq, ATT_TQ)
    nseg = len(kv_segs)
    hw = 2 * DA_HEAD_DIM
    specs = [pl.BlockSpec((4, DA_HEAD_DIM), lambda i, h, m: (0, 0)),
             pl.BlockSpec((1, hw), lambda i, h, m: (0, 0)),
             pl.BlockSpec((1, tq, hw), lambda i, h, m: (i, m, q_blk0 + h))]
    args = [lam_p.astype(F32), norm_g.reshape(1, hw), q_arr]
    for k_arr, k_blk0, _, _ in kv_segs:
        specs.append(pl.BlockSpec((1, k_arr.shape[1], hw), functools.partial(
            lambda i, h, m, o: (i, 0, o + h), o=k_blk0)))
        args.append(k_arr)
    for _, _, v_arr, v_blk0 in kv_segs:
        specs.append(pl.BlockSpec((1, v_arr.shape[1], hw), functools.partial(
            lambda i, h, m, o: (i, 0, o + h), o=v_blk0)))
        args.append(v_arr)
    return pl.pallas_call(
        functools.partial(_attn_body, nseg=nseg, lam_init=lam_init),
        grid=(b, DA_HEADS, lq // tq),
        in_specs=specs,
        out_specs=pl.BlockSpec((1, tq, hw), lambda i, h, m: (i, m, h)),
        out_shape=jax.ShapeDtypeStruct((b, lq, DA_WIDTH), BF16),
        compiler_params=_cparams("parallel", "parallel", "arbitrary"),
        name="diff_attention",
    )(*args)


def _merge_body(s_ref, h_ref, a_ref, g0_ref, g1_ref, g2_ref, wb_ref, wo_ref, x_ref, m_ref, ng_ref, o_ref):
    mixed = None
    for br, gl, i in ((s_ref, g0_ref, 0), (h_ref, g1_ref, 1), (a_ref, g2_ref, 2)):
        t = jax.nn.sigmoid(gl[0].astype(F32)) * jnp.dot(br[0], wb_ref[i], preferred_element_type=F32)
        mixed = t if mixed is None else mixed + t
    out = jnp.dot(mixed.astype(BF16), wo_ref[...], preferred_element_type=F32)
    o_ref[0] = x_ref[0] + m_ref[0] * _rms(out, ng_ref[...])


def _merge_residual(ssd, hy, da, big, w_branch, w_out, x, mod_gate, norm_g):
    b, l, d = x.shape
    tm = min(l, 512)
    row = lambda col: pl.BlockSpec((1, tm, d), lambda i, m: (i, m, col))
    return pl.pallas_call(
        _merge_body,
        grid=(b, l // tm),
        in_specs=[row(0), row(0), row(0), row(BLK_GATE), row(BLK_GATE + 1), row(BLK_GATE + 2),
                  pl.BlockSpec((3, d, d), lambda i, m: (0, 0, 0)),
                  pl.BlockSpec((d, d), lambda i, m: (0, 0)),
                  row(0),
                  pl.BlockSpec((1, 1, d), lambda i, m: (i, 0, 0)),
                  pl.BlockSpec((1, d), lambda i, m: (0, 0))],
        out_specs=row(0),
        out_shape=jax.ShapeDtypeStruct((b, l, d), F32),
        compiler_params=_cparams("parallel", "parallel"),
        name="merge_residual",
    )(ssd, hy, da, big, big, big, w_branch, w_out, x, mod_gate.reshape(b, 1, d), norm_g.reshape(1, d))


def _ffn_down_body(gt_ref, p_ref, n_ref, up_ref, cw_ref, cb_ref, wd_ref, x_ref, m_ref, ng_ref, o_ref):
    m, nm = pl.program_id(1), pl.num_programs(1)
    gate = _conv_rows(gt_ref[0].astype(F32), p_ref[0].astype(F32), n_ref[0].astype(F32),
                      cw_ref[...], cb_ref[...], m, nm)
    act = (_silu(gate) * up_ref[0].astype(F32)).astype(BF16)
    out = jnp.dot(act, wd_ref[...], preferred_element_type=F32)
    o_ref[0] = x_ref[0] + m_ref[0] * _rms(out, ng_ref[...])


def _ffn_down_residual(up, conv_w, conv_b, w_down, x, mod_gate, norm_g):
    b, l, d = x.shape
    f = D_FF
    tm = min(l, 512)
    prev, nxt = _halo_specs(tm, l, f, 0)
    return pl.pallas_call(
        _ffn_down_body,
        grid=(b, l // tm),
        in_specs=[pl.BlockSpec((1, tm, f), lambda i, m: (i, m, 0)), prev, nxt,
                  pl.BlockSpec((1, tm, f), lambda i, m: (i, m, 1)),
                  pl.BlockSpec((FFN_CONV, f), lambda i, m: (0, 0)),
                  pl.BlockSpec((1, f), lambda i, m: (0, 0)),
                  pl.BlockSpec((f, d), lambda i, m: (0, 0)),
                  pl.BlockSpec((1, tm, d), lambda i, m: (i, m, 0)),
                  pl.BlockSpec((1, 1, d), lambda i, m: (i, 0, 0)),
                  pl.BlockSpec((1, d), lambda i, m: (0, 0))],
        out_specs=pl.BlockSpec((1, tm, d), lambda i, m: (i, m, 0)),
        out_shape=jax.ShapeDtypeStruct((b, l, d), F32),
        compiler_params=_cparams("parallel", "parallel"),
        name="ffn_down_residual",
    )(up, up, up, up, conv_w.astype(F32), conv_b.astype(F32).reshape(1, f), w_down, x,
      mod_gate.reshape(b, 1, d), norm_g.reshape(1, d))


def _project(x, g, shift, scale, w_big, w_xbc, w_dt):
    big = _norm_mod_matmul(x, g, shift, scale, w_big, 1024, BF16, "proj_big")
    xbc = _norm_mod_matmul(x, g, shift, scale, w_xbc, 512, BF16, "proj_xbc")
    dt = _norm_mod_matmul(x, g, shift, scale, w_dt, LANES, F32, "proj_dt")
    return big, xbc, dt


def _lane_row(v):
    flat = v.astype(F32).reshape(1, 2 * SSD_HEADS)
    return jnp.pad(flat, ((0, 0), (0, LANES - 2 * SSD_HEADS)))


def kernel(x, c, ctx, c_ctx, w_ada, b_ada, norm_g, w_in, ssd_conv_w, ssd_conv_b, ssd_a_log, ssd_dt_bias, ssd_d, ssd_norm, hy_conv_w, hy_conv_b, hy_w1, hy_b1, hy_f1, hy_w2, hy_b2, hy_f2, hy_w3, hy_bias, da_lambda, da_norm, w_branch, w_out, ffn_w_up, ffn_conv_w, ffn_conv_b, ffn_w_down):
    b, seq, d = x.shape
    depth = w_ada.shape[0]
    cos_t, sin_t = _rope_tables(seq)
    mod = _ada_mod(c, c_ctx, w_ada, b_ada)
    x_l, x_c = x, ctx
    hshape = (b, SSD_GROUPS, SSD_STATE, SSD_GROUP_W)
    for i in range(depth):
        ctx_out = i < depth - 1
        ml = [mod[i, :b, k * d:(k + 1) * d] for k in range(6)]
        mc = [jnp.broadcast_to(mod[i, b:b + 1, k * d:(k + 1) * d], (b, d)) for k in range(6)]
        ng = norm_g[i]
        wi = w_in[i]
        w_big = jnp.concatenate([wi[:, :SSD_INNER], wi[:, OFF_HY:]], axis=1).astype(BF16)
        w_xbc = wi[:, SSD_INNER:SSD_INNER + SSD_CONV_CH].astype(BF16)
        w_dt = jnp.pad(wi[:, SSD_INNER + SSD_CONV_CH:IN_SSD], ((0, 0), (0, LANES - 2 * SSD_HEADS))).astype(BF16)
        big_c, xbc_c, dt_c = _project(x_c, ng[0], mc[0], mc[1], w_big, w_xbc, w_dt)
        big_l, xbc_l, dt_l = _project(x_l, ng[0], ml[0], ml[1], w_big, w_xbc, w_dt)

        conv_w = ssd_conv_w[i].astype(F32)
        dtb, alog = _lane_row(ssd_dt_bias[i]), _lane_row(ssd_a_log[i])
        dskip_x = jnp.repeat(ssd_d[i].astype(F32), SSD_HEAD_DIM).reshape(1, SSD_INNER)
        h0 = jnp.zeros(hshape, F32)
        ssd_c, hf, hb = _ssd_branch(_ssd_prep(xbc_c, conv_w, ssd_conv_b[i].astype(F32)), dt_c, big_c,
                                    dtb, alog, dskip_x, ssd_norm[i], h0, h0)
        ssd_l, _, _ = _ssd_branch(_ssd_prep(xbc_l, conv_w, ssd_conv_b[i].astype(F32)), dt_l, big_l,
                                  dtb, alog, dskip_x, ssd_norm[i], hf, hb)

        filt = (hy_w1[i], hy_b1[i], hy_f1[i], hy_w2[i], hy_b2[i], hy_f2[i], hy_w3[i])
        hy_l = _hyena_branch(big_l, hy_conv_w[i], hy_conv_b[i], hy_bias[i].astype(F32), filt, True)

        lam_init = 0.8 - 0.6 * math.exp(-0.3 * i)
        qk_l = _rope_qk(big_l, cos_t, sin_t)
        hpb = D_MODEL // (2 * DA_HEAD_DIM)
        segs = [(big_c, BLK_K * hpb, big_c, BLK_V * hpb), (qk_l, hpb, big_l, BLK_V * hpb)]
        da_l = _diff_attention(qk_l, 0, segs, da_lambda[i], da_norm[i], lam_init)

        wb = w_branch[i].astype(BF16)
        wo = w_out[i].astype(BF16)
        w_up = ffn_w_up[i].astype(BF16)
        w_dn = ffn_w_down[i].astype(BF16)
        x_l = _merge_residual(ssd_l, hy_l, da_l, big_l, wb, wo, x_l, ml[2], ng[1])
        up_l = _norm_mod_matmul(x_l, ng[2], ml[3], ml[4], w_up, 1408, BF16, "ffn_up")
        x_l = _ffn_down_residual(up_l, ffn_conv_w[i], ffn_conv_b[i], w_dn, x_l, ml[5], ng[3])
        if ctx_out:
            hy_c = _hyena_branch(big_c, hy_conv_w[i], hy_conv_b[i], hy_bias[i].astype(F32), filt, False)
            da_c = _diff_attention(big_c, BLK_Q * hpb, [(big_c, BLK_K * hpb, big_c, BLK_V * hpb)],
                                   da_lambda[i], da_norm[i], lam_init)
            x_c = _merge_residual(ssd_c, hy_c, da_c, big_c, wb, wo, x_c, mc[2], ng[1])
            up_c = _norm_mod_matmul(x_c, ng[2], mc[3], mc[4], w_up, 1408, BF16, "ffn_up")
            x_c = _ffn_down_residual(up_c, ffn_conv_w[i], ffn_conv_b[i], w_dn, x_c, mc[5], ng[3])
    return x_l
```

```python
import functools
import math

import numpy as np
import jax
import jax.numpy as jnp
from jax import lax
from jax.experimental import pallas as pl
from jax.experimental.pallas import tpu as pltpu

F32 = jnp.float32
BF16 = jnp.bfloat16
HIGHEST = lax.Precision.HIGHEST

D_MODEL = 1024
GRID_W = 64
EPS = 1e-6

SSD_INNER = D_MODEL
SSD_HEAD_DIM = 64
SSD_HEADS = SSD_INNER // SSD_HEAD_DIM
SSD_GROUPS = 2
SSD_STATE = 128
SSD_CONV = 5
SSD_CHUNK = 128
SSD_CONV_CH = SSD_INNER + 2 * SSD_GROUPS * SSD_STATE
SSD_GROUP_W = SSD_INNER // SSD_GROUPS

HY_WIDTH = D_MODEL
HY_ORDER = 2
HY_SHORT = 3
HY_BANDS = 16
HY_EMB = 1 + 2 * HY_BANDS
HY_FF = 64
HY_TARGET = 1e-2
HY_FAST = 0.3
HY_SLOW = 1.5
HY_N2 = 64
HY_CB = 128
HY_CB1 = 512
HY_PAD = 8
HY_UNROLL = 16

DA_HEAD_DIM = 64
DA_HEADS = D_MODEL // (2 * DA_HEAD_DIM)
DA_WIDTH = DA_HEADS * 2 * DA_HEAD_DIM
ROPE_BASE = 10000.0
ATT_TQ = 512
ATT_SUB = 256

D_FF = ((8 * D_MODEL // 3 + 127) // 128) * 128
FFN_CONV = 3

IN_SSD = SSD_INNER + SSD_CONV_CH + 2 * SSD_HEADS
IN_HY = (HY_ORDER + 1) * HY_WIDTH
IN_DA = 3 * DA_WIDTH
OFF_HY = IN_SSD
OFF_DA = OFF_HY + IN_HY
OFF_GATE = OFF_DA + IN_DA

BLK_Z, BLK_HY, BLK_Q, BLK_K, BLK_V, BLK_GATE = 0, 1, 4, 5, 6, 7
BIG_COLS = 10 * D_MODEL

VMEM_LIMIT = 56 * 1024 * 1024
PROJ_TM = 2048
FFN_TM = 1024
LANES = 128
HALO = 16


def _cparams(*sem):
    return pltpu.CompilerParams(dimension_semantics=sem, vmem_limit_bytes=VMEM_LIMIT)


def _silu(x):
    return x * jax.nn.sigmoid(x)


def _rms(x, g):
    return x * lax.rsqrt(jnp.mean(x * x, axis=-1, keepdims=True) + EPS) * g


def _round_up(a, m):
    return (a + m - 1) // m * m


def _pitch(rows):
    p8 = _round_up(rows, 8) // 8
    return 8 * (p8 if p8 % 2 else p8 + 1)


def _ada_body(s_ref, w_ref, b_ref, o_ref):
    s = _silu(s_ref[...])
    o_ref[0] = jnp.dot(s, w_ref[0], preferred_element_type=F32, precision=HIGHEST) + b_ref[0]


def _ada_mod(c, c_ctx, w_ada, b_ada):
    depth, d, n6 = w_ada.shape
    b = c.shape[0]
    rows = _round_up(b + 1, 8)
    s = jnp.zeros((rows, d), F32).at[:b].set(c).at[b].set(c_ctx)
    tn = 1536
    return pl.pallas_call(
        _ada_body,
        grid=(depth, n6 // tn),
        in_specs=[pl.BlockSpec((rows, d), lambda l, j: (0, 0)),
                  pl.BlockSpec((1, d, tn), lambda l, j: (l, 0, j)),
                  pl.BlockSpec((1, 1, tn), lambda l, j: (l, 0, j))],
        out_specs=pl.BlockSpec((1, rows, tn), lambda l, j: (l, 0, j)),
        out_shape=jax.ShapeDtypeStruct((depth, rows, n6), F32),
        compiler_params=_cparams("parallel", "parallel"),
        name="ada_mod",
    )(s, w_ada, b_ada.reshape(depth, 1, n6))


def _nmm_body(x_ref, g_ref, sh_ref, sc_ref, w_ref, o_ref, h_scr):
    @pl.when(pl.program_id(2) == 0)
    def _():
        h = _rms(x_ref[0], g_ref[...]) * (1.0 + sc_ref[0]) + sh_ref[0]
        h_scr[...] = h.astype(BF16)

    o_ref[0] = jnp.dot(h_scr[...], w_ref[...], preferred_element_type=F32).astype(o_ref.dtype)


def _norm_mod_matmul(x, g, shift, scale, w, tm, tn, out_dtype, name):
    b, l, d = x.shape
    n = w.shape[1]
    tm = min(l, tm)
    return pl.pallas_call(
        _nmm_body,
        grid=(b, l // tm, n // tn),
        in_specs=[pl.BlockSpec((1, tm, d), lambda i, m, j: (i, m, 0)),
                  pl.BlockSpec((1, d), lambda i, m, j: (0, 0)),
                  pl.BlockSpec((1, 1, d), lambda i, m, j: (i, 0, 0)),
                  pl.BlockSpec((1, 1, d), lambda i, m, j: (i, 0, 0)),
                  pl.BlockSpec((d, tn), lambda i, m, j: (0, j))],
        out_specs=pl.BlockSpec((1, tm, tn), lambda i, m, j: (i, m, j)),
        out_shape=jax.ShapeDtypeStruct((b, l, n), out_dtype),
        scratch_shapes=[pltpu.VMEM((tm, d), BF16)],
        compiler_params=_cparams("parallel", "parallel", "arbitrary"),
        name=name,
    )(x, g.reshape(1, d), shift.reshape(b, 1, d), scale.reshape(b, 1, d), w)


def _halo_specs(tl, l, c, col):
    per = tl // HALO
    last = l // HALO - 1
    prev = pl.BlockSpec((1, HALO, c), lambda i, m: (i, jnp.maximum(m * per - 1, 0), col))
    nxt = pl.BlockSpec((1, HALO, c), lambda i, m: (i, jnp.minimum((m + 1) * per, last), col))
    return prev, nxt


def _conv_rows(x, prev, nxt, w, bias, m, nm):
    tl = x.shape[0]
    k = w.shape[0]
    pad = k // 2
    prev = prev * (m > 0).astype(F32)
    nxt = nxt * (m < nm - 1).astype(F32)
    ext = jnp.concatenate([prev, x, nxt], axis=0)
    rows = tl + 2 * HALO
    y = bias
    for j in range(k):
        shifted = ext if j == pad else pltpu.roll(ext, (pad - j) % rows, 0)
        y = y + shifted[HALO:HALO + tl] * w[j:j + 1]
    return y


def _ssd_prep_body(x_ref, p_ref, n_ref, w_ref, b_ref, o_ref):
    m, nm = pl.program_id(1), pl.num_programs(1)
    y = _conv_rows(x_ref[0].astype(F32), p_ref[0].astype(F32), n_ref[0].astype(F32),
                   w_ref[...], b_ref[...], m, nm)
    o_ref[0] = _silu(y).astype(o_ref.dtype)


def _ssd_prep(xbc_raw, conv_w, conv_b):
    b, l, c = xbc_raw.shape
    tl = min(l, 512)
    prev, nxt = _halo_specs(tl, l, c, 0)
    return pl.pallas_call(
        _ssd_prep_body,
        grid=(b, l // tl),
        in_specs=[pl.BlockSpec((1, tl, c), lambda i, m: (i, m, 0)), prev, nxt,
                  pl.BlockSpec((SSD_CONV, c), lambda i, m: (0, 0)),
                  pl.BlockSpec((1, c), lambda i, m: (0, 0))],
        out_specs=pl.BlockSpec((1, tl, c), lambda i, m: (i, m, 0)),
        out_shape=jax.ShapeDtypeStruct((b, l, c), BF16),
        compiler_params=_cparams("parallel", "parallel"),
        name="ssd_prep",
    )(xbc_raw, xbc_raw, xbc_raw, conv_w, conv_b.reshape(1, c))


def _softplus(x):
    return jnp.maximum(x, 0.0) + jnp.log1p(jnp.exp(-jnp.abs(x)))


def _ssd_chunk(xbc, dt_raw, dtb, alog, h_scr, reverse, col0):
    cs = xbc.shape[0]
    gw = SSD_GROUPS * SSD_STATE
    x = xbc[:, :SSD_INNER].astype(F32)
    dt = _softplus(dt_raw + dtb)
    da = dt * (-jnp.exp(alog))
    row = lax.broadcasted_iota(jnp.int32, (cs, cs), 0)
    col = lax.broadcasted_iota(jnp.int32, (cs, cs), 1)
    keep = (row <= col) if reverse else (row >= col)
    acum = jnp.dot(keep.astype(F32), da, preferred_element_type=F32, precision=HIGHEST)
    acum_t = acum.T
    head_of_lane = col0 + lax.broadcasted_iota(jnp.int32, (2 * LANES, SSD_INNER), 1) // SSD_HEAD_DIM
    src_lane = lax.broadcasted_iota(jnp.int32, (2 * LANES, SSD_INNER), 0) % LANES
    expand = (src_lane == head_of_lane).astype(BF16)

    def per_head_lanes(v):
        hi = v.astype(BF16)
        lo = (v - hi.astype(F32)).astype(BF16)
        return jnp.dot(jnp.concatenate([hi, lo], axis=1), expand, preferred_element_type=F32)

    end = 0 if reverse else cs - 1
    into = per_head_lanes(jnp.exp(acum))
    carry_w = per_head_lanes(jnp.exp(acum[end:end + 1] - acum))
    dec = into[end:end + 1]
    xdt = x * per_head_lanes(dt)
    xw = (xdt * carry_w).astype(BF16)
    xdt_b = xdt.astype(BF16)
    lane = lax.broadcasted_iota(jnp.int32, (cs, LANES), 1)
    parts = []
    for g in range(SSD_GROUPS):
        bm = xbc[:, SSD_INNER + g * SSD_STATE:SSD_INNER + (g + 1) * SSD_STATE]
        cm = xbc[:, SSD_INNER + gw + g * SSD_STATE:SSD_INNER + gw + (g + 1) * SSD_STATE]
        cb = lax.dot_general(cm, bm, (((1,), (1,)), ((), ())), preferred_element_type=F32)
        sl = slice(g * SSD_GROUP_W, (g + 1) * SSD_GROUP_W)
        h_t = h_scr[g]
        y_off = jnp.dot(cm, h_t.astype(BF16), preferred_element_type=F32) * into[:, sl]
        bm_t = bm.astype(F32).T.astype(BF16)
        h_scr[g] = h_t * dec[:, sl] + jnp.dot(bm_t, xw[:, sl], preferred_element_type=F32)
        for p in range(SSD_GROUP_W // LANES):
            lo = g * SSD_GROUP_W + p * LANES
            xp = xdt_b[:, lo:lo + LANES]
            ys = []
            for q in range(LANES // SSD_HEAD_DIM):
                hc = col0 + lo // SSD_HEAD_DIM + q
                seg = acum[:, hc:hc + 1] - acum_t[hc:hc + 1, :]
                dmat = (cb * jnp.exp(jnp.where(keep, seg, -1e30))).astype(BF16)
                ys.append(jnp.dot(dmat, xp, preferred_element_type=F32))
            y_diag = jnp.where(lane < SSD_HEAD_DIM, ys[0], ys[1])
            parts.append(y_diag + y_off[:, p * LANES:(p + 1) * LANES])
    return jnp.concatenate(parts, axis=1), x


def _ssd_fwd_body(xbc_ref, dt_ref, dtb_ref, alog_ref, h0_ref, y_ref, hl_ref, h_scr):
    c = pl.program_id(1)

    @pl.when(c == 0)
    def _():
        h_scr[...] = h0_ref[0]

    y, _ = _ssd_chunk(xbc_ref[0], dt_ref[0], dtb_ref[...], alog_ref[...], h_scr, False, 0)
    y_ref[0] = y

    @pl.when(c == pl.num_programs(1) - 1)
    def _():
        hl_ref[0] = h_scr[...]


def _ssd_bwd_body(xbc_ref, dt_ref, dtb_ref, alog_ref, h0_ref, yf_ref, z_ref, dsk_ref, ng_ref,
                  o_ref, hl_ref, h_scr):
    c = pl.program_id(1)

    @pl.when(c == 0)
    def _():
        h_scr[...] = h0_ref[0]

    yb, x = _ssd_chunk(xbc_ref[0], dt_ref[0], dtb_ref[...], alog_ref[...], h_scr, True, SSD_HEADS)
    y = yf_ref[0] + yb + x * dsk_ref[...]
    o_ref[0] = _rms(y * _silu(z_ref[0].astype(F32)), ng_ref[...]).astype(o_ref.dtype)

    @pl.when(c == pl.num_programs(1) - 1)
    def _():
        hl_ref[0] = h_scr[...]


def _ssd_branch(xbc_act, dt_raw, big, dtb, alog, dskip_x, norm_g, h0f, h0b):
    b, l, _ = xbc_act.shape
    cs = SSD_CHUNK
    nc = l // cs
    hshape = (SSD_GROUPS, SSD_STATE, SSD_GROUP_W)
    state_spec = pl.BlockSpec((1,) + hshape, lambda i, c: (i, 0, 0, 0))
    vec = lambda w: pl.BlockSpec((1, w), lambda i, c: (0, 0))
    state_shape = jax.ShapeDtypeStruct((b,) + hshape, F32)
    yf, hf = pl.pallas_call(
        _ssd_fwd_body,
        grid=(b, nc),
        in_specs=[pl.BlockSpec((1, cs, SSD_CONV_CH), lambda i, c: (i, c, 0)),
                  pl.BlockSpec((1, cs, LANES), lambda i, c: (i, c, 0)),
                  vec(LANES), vec(LANES), state_spec],
        out_specs=[pl.BlockSpec((1, cs, SSD_INNER), lambda i, c: (i, c, 0)), state_spec],
        out_shape=[jax.ShapeDtypeStruct((b, l, SSD_INNER), F32), state_shape],
        scratch_shapes=[pltpu.VMEM(hshape, F32)],
        compiler_params=_cparams("parallel", "arbitrary"),
        name="ssd_scan_fwd",
    )(xbc_act, dt_raw, dtb, alog, h0f)
    rev = lambda i, c: (i, nc - 1 - c, 0)
    out, hb = pl.pallas_call(
        _ssd_bwd_body,
        grid=(b, nc),
        in_specs=[pl.BlockSpec((1, cs, SSD_CONV_CH), rev),
                  pl.BlockSpec((1, cs, LANES), rev),
                  vec(LANES), vec(LANES), state_spec,
                  pl.BlockSpec((1, cs, SSD_INNER), rev),
                  pl.BlockSpec((1, cs, SSD_INNER), lambda i, c: (i, nc - 1 - c, BLK_Z)),
                  vec(SSD_INNER), vec(SSD_INNER)],
        out_specs=[pl.BlockSpec((1, cs, SSD_INNER), rev), state_spec],
        out_shape=[jax.ShapeDtypeStruct((b, l, SSD_INNER), BF16), state_shape],
        scratch_shapes=[pltpu.VMEM(hshape, F32)],
        compiler_params=_cparams("parallel", "arbitrary"),
        name="ssd_scan_bwd",
    )(xbc_act, dt_raw, dtb, alog, h0b, yf, big, dskip_x, norm_g.reshape(1, SSD_INNER))
    return out, hf, hb


def _hy_filter_body(z_ref, w1_ref, b1_ref, f1_ref, w2_ref, b2_ref, f2_ref, w3_ref, dl_ref, o_ref, *, n, tr):
    z = z_ref[...]
    h = jnp.sin(f1_ref[...] * (jnp.dot(z, w1_ref[...], preferred_element_type=F32, precision=HIGHEST)
                               + b1_ref[...]))
    h = jnp.sin(f2_ref[...] * (jnp.dot(h, w2_ref[...], preferred_element_type=F32, precision=HIGHEST)
                               + b2_ref[...]))
    filt = jnp.dot(h.astype(BF16), w3_ref[...], preferred_element_type=F32)
    filt = filt * jnp.exp(-z[:, 0:1] * dl_ref[...])
    rows = pl.program_id(0) * tr + lax.broadcasted_iota(jnp.int32, (tr, 1), 0)
    o_ref[...] = jnp.where(rows == n, 0.0, filt)


def _hy_features(n):
    t = jnp.linspace(0.0, 1.0, n, dtype=F32)[:, None]
    w = (2.0 * math.pi / n) * jnp.arange(n, dtype=F32)[:, None]
    bands = jnp.linspace(1e-4, HY_BANDS - 1, HY_BANDS, dtype=F32)
    z = jnp.concatenate([t, jnp.cos(bands * w), -jnp.sin(bands * w)], axis=-1)
    return jnp.pad(z, ((0, 0), (0, LANES - HY_EMB)))


def _hy_filter(n, w1, b1, f1, w2, b2, f2, w3):
    z = _hy_features(n)
    zk = jnp.concatenate([z, z[:1], jnp.flip(z[1:], axis=0)], axis=0)
    padw = lambda a, r, c: jnp.pad(a.astype(F32), ((0, r - a.shape[0]), (0, c - a.shape[1])))
    w1p = padw(w1, LANES, LANES)
    w2p = padw(w2, LANES, LANES)
    w3p = padw(w3, LANES, w3.shape[1]).astype(BF16)
    rowp = lambda a: padw(a.reshape(1, -1), 1, LANES)
    deltas = jnp.abs(jnp.linspace(math.log(HY_TARGET) / HY_SLOW, math.log(HY_TARGET) / HY_FAST,
                                  HY_WIDTH, dtype=F32))
    dl = jnp.tile(deltas, HY_ORDER).reshape(1, HY_ORDER * HY_WIDTH)
    tr = min(n, 512)
    wide = HY_ORDER * HY_WIDTH
    full = lambda r, c: pl.BlockSpec((r, c), lambda i: (0, 0))
    return pl.pallas_call(
        functools.partial(_hy_filter_body, n=n, tr=tr),
        grid=(2 * n // tr,),
        in_specs=[pl.BlockSpec((tr, LANES), lambda i: (i, 0)),
                  full(LANES, LANES), full(1, LANES), full(1, LANES),
                  full(LANES, LANES), full(1, LANES), full(1, LANES),
                  pl.BlockSpec((LANES, wide), lambda i: (0, (i * tr) // n)),
                  full(1, wide)],
        out_specs=pl.BlockSpec((tr, wide), lambda i: (i, 0)),
        out_shape=jax.ShapeDtypeStruct((2 * n, wide), F32),
        compiler_params=_cparams("parallel"),
        name="hy_filter",
    )(zk, w1p, rowp(b1), rowp(f1), w2p, rowp(b2), rowp(f2), w3p, dl)


def _two_stage_tables(n, filt_rows):
    big_n = 2 * n
    n2 = HY_N2
    n1 = big_n // n2
    k1n = n1 // 2 + 1
    p = _round_up(k1n, 8)
    s2 = np.arange(n2)
    k1 = np.arange(k1n)

    def first(s_rows):
        s1 = np.arange(s_rows)
        ang = 2.0 * np.pi * ((k1[None, :, None] * (n2 * s1[None, None, :] + s2[:, None, None])) % big_n) / big_n
        out = np.zeros((n2, 2 * p, s_rows), np.float32)
        out[:, :k1n] = np.cos(ang)
        out[:, p:p + k1n] = -np.sin(ang)
        return out

    ang2 = 2.0 * np.pi * ((s2[:, None] * s2[None, :]) % n2) / n2
    c2, sn2 = np.cos(ang2), np.sin(ang2)
    second = np.block([[c2, sn2], [-sn2, c2]]).astype(np.float32)
    second_inv = np.block([[c2, -sn2], [sn2, c2]]).astype(np.float32)
    s1o = np.arange(n // n2)
    wgt = np.where((k1 == 0) | (k1 == n1 // 2), 1.0, 2.0) / big_n
    ang = 2.0 * np.pi * ((k1[None, None, :] * (n2 * s1o[None, :, None] + s2[:, None, None])) % big_n) / big_n
    last = np.zeros((n2, n // n2, 2 * p), np.float32)
    last[:, :, :k1n] = wgt * np.cos(ang)
    last[:, :, p:p + k1n] = -wgt * np.sin(ang)
    cast = lambda a: jnp.asarray(a).astype(BF16)
    return dict(k1n=k1n, p=p, first=cast(first(n // n2)), first_filt=cast(first(filt_rows // n2)),
                second=cast(second), second_inv=cast(second_inv), last=cast(last))


def _one_stage_tables(n):
    big_n = 2 * n
    kn = n + 1
    p = _round_up(kn, 8)
    k = np.arange(kn)

    def fwd(rows):
        s = np.arange(rows)
        ang = 2.0 * np.pi * ((k[:, None] * s[None, :]) % big_n) / big_n
        out = np.zeros((2 * p, rows), np.float32)
        out[:kn] = np.cos(ang)
        out[p:p + kn] = -np.sin(ang)
        return out

    s = np.arange(n)
    wgt = np.where((k == 0) | (k == n), 1.0, 2.0) / big_n
    ang = 2.0 * np.pi * ((s[:, None] * k[None, :]) % big_n) / big_n
    inv = np.zeros((n, 2 * p), np.float32)
    inv[:, :kn] = wgt * np.cos(ang)
    inv[:, p:p + kn] = -wgt * np.sin(ang)
    cast = lambda a: jnp.asarray(a).astype(BF16)
    return dict(p=p, fwd=cast(fwd(n)), fwd_filt=cast(fwd(big_n)), inv=cast(inv))


def _spec2_body(k_ref, f1_ref, f2_ref, o_ref, a_scr, *, k1n, p, s_rows):
    n2 = HY_N2
    ap = _pitch(2 * p)

    def stage1(s2, carry):
        rows = k_ref[pl.ds(s2, s_rows, stride=n2), :]
        a = jnp.dot(f1_ref[s2], rows.astype(BF16), preferred_element_type=F32)
        a_scr[pl.ds(pl.multiple_of(s2 * ap, 8), 2 * p), :] = a
        return carry

    lax.fori_loop(0, n2, stage1, 0, unroll=HY_UNROLL)

    def stage2(k1, carry):
        re = a_scr[pl.ds(k1, n2, stride=ap), :]
        im = a_scr[pl.ds(p + k1, n2, stride=ap), :]
        a = jnp.concatenate([re, im], axis=0).astype(BF16)
        o_ref[k1] = jnp.dot(f2_ref[...], a, preferred_element_type=F32)
        return carry

    lax.fori_loop(0, k1n, stage2, 0, unroll=HY_UNROLL)


def _hy_spectrum2(kfilt, tabs):
    rows, wide = kfilt.shape
    k1n, p = tabs["k1n"], tabs["p"]
    s_rows = rows // HY_N2
    cb = HY_CB
    return pl.pallas_call(
        functools.partial(_spec2_body, k1n=k1n, p=p, s_rows=s_rows),
        grid=(wide // cb,),
        in_specs=[pl.BlockSpec((rows, cb), lambda j: (0, j)),
                  pl.BlockSpec((HY_N2, 2 * p, s_rows), lambda j: (0, 0, 0)),
                  pl.BlockSpec((2 * HY_N2, 2 * HY_N2), lambda j: (0, 0))],
        out_specs=pl.BlockSpec((k1n, 2 * HY_N2, cb), lambda j: (0, 0, j)),
        out_shape=jax.ShapeDtypeStruct((k1n, 2 * HY_N2, wide), F32),
        scratch_shapes=[pltpu.VMEM((HY_N2 * _pitch(2 * p), cb), F32)],
        compiler_params=_cparams("parallel"),
        name="hy_spectrum2",
    )(kfilt, tabs["first_filt"], tabs["second"])


def _spec1_body(k_ref, f_ref, o_ref):
    o_ref[...] = jnp.dot(f_ref[...], k_ref[...].astype(BF16), preferred_element_type=F32)


def _hy_spectrum1(kfilt, tabs):
    rows, wide = kfilt.shape
    p = tabs["p"]
    cb = HY_CB1
    return pl.pallas_call(
        _spec1_body,
        grid=(wide // cb,),
        in_specs=[pl.BlockSpec((rows, cb), lambda j: (0, j)),
                  pl.BlockSpec((2 * p, rows), lambda j: (0, 0))],
        out_specs=pl.BlockSpec((2 * p, cb), lambda j: (0, j)),
        out_shape=jax.ShapeDtypeStruct((2 * p, wide), F32),
        compiler_params=_cparams("parallel"),
        name="hy_spectrum1",
    )(kfilt, tabs["fwd_filt"])


def _fill_pad(src_ref, pad_scr, n, rb):
    for r in range(0, n, rb):
        pad_scr[pl.ds(HY_PAD + r, rb), :] = src_ref[0, pl.ds(r, rb), :].astype(F32)


def _zero_pad_rows(pad_scr, n):
    zeros = jnp.zeros((HY_PAD, pad_scr.shape[1]), F32)
    pad_scr[pl.ds(0, HY_PAD), :] = zeros
    pad_scr[pl.ds(HY_PAD + n, HY_PAD), :] = zeros


def _conv3_group(pad_scr, w_ref, b_ref, g, grp):
    acc = b_ref[...]
    for j in range(HY_SHORT):
        acc = acc + pad_scr[pl.ds(HY_PAD - 1 + j + g * grp, grp), :] * w_ref[j:j + 1, :]
    return acc


def _load_u(u_ref, cwu_ref, cbu_ref, pad_scr, u_scr, n, rb, conv_u, grp, gp):
    _fill_pad(u_ref, pad_scr, n, rb)
    for g in range(n // grp):
        if conv_u:
            val = _conv3_group(pad_scr, cwu_ref, cbu_ref, g, grp)
        else:
            val = pad_scr[pl.ds(HY_PAD + g * grp, grp), :]
        u_scr[pl.ds(g * gp, grp), :] = val


def _gate_out(g_ref, cwg_ref, cbg_ref, bias_ref, pad_scr, u_scr, y_scr, o_ref, n, rb, grp, gp):
    _fill_pad(g_ref, pad_scr, n, rb)
    for g in range(n // grp):
        gate = _conv3_group(pad_scr, cwg_ref, cbg_ref, g, grp)
        rows = pl.ds(g * gp, grp)
        y = y_scr[rows, :] + u_scr[rows, :] * bias_ref[...]
        o_ref[0, pl.ds(g * grp, grp), :] = (gate * y).astype(o_ref.dtype)


def _conv2_body(u_ref, g_ref, cwu_ref, cbu_ref, cwg_ref, cbg_ref, bias_ref, kf_ref,
                f1_ref, f2_ref, i1_ref, l_ref, o_ref, pad_scr, u_scr, y_scr, a_scr, b_scr,
                *, n, k1n, p, conv_u):
    n2 = HY_N2
    s_rows = n // n2
    rb = min(n, 512)
    gp, ap, bp = _pitch(n2), _pitch(2 * p), _pitch(2 * n2)
    _zero_pad_rows(pad_scr, n)
    _load_u(u_ref, cwu_ref, cbu_ref, pad_scr, u_scr, n, rb, conv_u, n2, gp)

    def stage1(s2, carry):
        rows = u_scr[pl.ds(s2, s_rows, stride=gp), :]
        a = jnp.dot(f1_ref[s2], rows.astype(BF16), preferred_element_type=F32)
        a_scr[pl.ds(pl.multiple_of(s2 * ap, 8), 2 * p), :] = a
        return carry

    lax.fori_loop(0, n2, stage1, 0, unroll=HY_UNROLL)

    for k1 in range(k1n, p):
        b_scr[pl.ds(k1 * bp, 2 * n2), :] = jnp.zeros((2 * n2, b_scr.shape[1]), F32)

    def stage2(k1, carry):
        re = a_scr[pl.ds(k1, n2, stride=ap), :]
        im = a_scr[pl.ds(p + k1, n2, stride=ap), :]
        x = jnp.dot(f2_ref[...], jnp.concatenate([re, im], axis=0).astype(BF16), preferred_element_type=F32)
        kf = kf_ref[k1]
        xr, xi, kr, ki = x[:n2], x[n2:], kf[:n2], kf[n2:]
        prod = jnp.concatenate([xr * kr - xi * ki, xr * ki + xi * kr], axis=0).astype(BF16)
        b_scr[pl.ds(pl.multiple_of(k1 * bp, 8), 2 * n2), :] = jnp.dot(
            i1_ref[...], prod, preferred_element_type=F32)
        return carry

    lax.fori_loop(0, k1n, stage2, 0, unroll=HY_UNROLL)

    def stage3(s2, carry):
        re = b_scr[pl.ds(s2, p, stride=bp), :]
        im = b_scr[pl.ds(n2 + s2, p, stride=bp), :]
        y = jnp.dot(l_ref[s2], jnp.concatenate([re, im], axis=0).astype(BF16), preferred_element_type=F32)
        y_scr[pl.ds(s2, s_rows, stride=gp), :] = y
        return carry

    lax.fori_loop(0, n2, stage3, 0, unroll=HY_UNROLL)
    _gate_out(g_ref, cwg_ref, cbg_ref, bias_ref, pad_scr, u_scr, y_scr, o_ref, n, rb, n2, gp)


def _conv1_body(u_ref, g_ref, cwu_ref, cbu_ref, cwg_ref, cbg_ref, bias_ref, kf_ref,
                f_ref, inv_ref, o_ref, pad_scr, u_scr, y_scr, *, n, p, conv_u):
    _zero_pad_rows(pad_scr, n)
    _load_u(u_ref, cwu_ref, cbu_ref, pad_scr, u_scr, n, n, conv_u, n, n)
    x = jnp.dot(f_ref[...], u_scr[...].astype(BF16), preferred_element_type=F32)
    kf = kf_ref[...]
    xr, xi, kr, ki = x[:p], x[p:], kf[:p], kf[p:]
    prod = jnp.concatenate([xr * kr - xi * ki, xr * ki + xi * kr], axis=0).astype(BF16)
    y_scr[...] = jnp.dot(inv_ref[...], prod, preferred_element_type=F32)
    _gate_out(g_ref, cwg_ref, cbg_ref, bias_ref, pad_scr, u_scr, y_scr, o_ref, n, n, n, n)


def _hy_long_conv(u_arr, u_blk, conv_u, g_arr, g_blk, conv_w, conv_b, bias, kf, order, tabs, two_stage):
    b, n, _ = g_arr.shape
    c = HY_WIDTH
    cb = HY_CB if two_stage else HY_CB1
    per = c // cb
    ublk = u_blk * per
    gblk = g_blk * per
    wu = ((u_blk - BLK_HY) if conv_u else 0) * per
    wg = (g_blk - BLK_HY) * per
    seq = lambda off: pl.BlockSpec((1, n, cb), lambda j, i: (i, 0, off + j))
    wrow = lambda rows, off: pl.BlockSpec((rows, cb), lambda j, i: (0, off + j))
    const = lambda shape: pl.BlockSpec(shape, lambda j, i: (0,) * len(shape))
    common = [seq(ublk), seq(gblk), wrow(HY_SHORT, wu), wrow(1, wu), wrow(HY_SHORT, wg), wrow(1, wg),
              wrow(1, order * per)]
    args = [u_arr, g_arr, conv_w, conv_b, conv_w, conv_b, bias.reshape(1, HY_ORDER * c)]
    seq_rows = (n // HY_N2) * _pitch(HY_N2) if two_stage else n
    scratch = [pltpu.VMEM((n + 2 * HY_PAD, cb), F32), pltpu.VMEM((seq_rows, cb), F32),
               pltpu.VMEM((seq_rows, cb), F32)]
    if two_stage:
        k1n, p = tabs["k1n"], tabs["p"]
        n2 = HY_N2
        body = functools.partial(_conv2_body, n=n, k1n=k1n, p=p, conv_u=conv_u)
        specs = common + [pl.BlockSpec((k1n, 2 * n2, cb), lambda j, i: (0, 0, order * per + j)),
                          const((n2, 2 * p, n // n2)), const((2 * n2, 2 * n2)), const((2 * n2, 2 * n2)),
                          const((n2, n // n2, 2 * p))]
        args += [kf, tabs["first"], tabs["second"], tabs["second_inv"], tabs["last"]]
        scratch += [pltpu.VMEM((n2 * _pitch(2 * p), cb), F32), pltpu.VMEM((p * _pitch(2 * n2), cb), F32)]
        name = "hy_conv2"
    else:
        p = tabs["p"]
        body = functools.partial(_conv1_body, n=n, p=p, conv_u=conv_u)
        specs = common + [pl.BlockSpec((2 * p, cb), lambda j, i: (0, order * per + j)),
                          const((2 * p, n)), const((n, 2 * p))]
        args += [kf, tabs["fwd"], tabs["inv"]]
        name = "hy_conv1"
    return pl.pallas_call(
        body,
        grid=(per, b),
        in_specs=specs,
        out_specs=pl.BlockSpec((1, n, cb), lambda j, i: (i, 0, j)),
        out_shape=jax.ShapeDtypeStruct((b, n, c), BF16),
        scratch_shapes=scratch,
        compiler_params=_cparams("parallel", "parallel"),
        name=name,
    )(*args)


def _hyena_branch(big, conv_w, conv_b, bias, filt_params, two_stage):
    b, n, _ = big.shape
    kfilt = _hy_filter(n, *filt_params)
    if two_stage:
        tabs = _two_stage_tables(n, 2 * n)
        kf = _hy_spectrum2(kfilt, tabs)
    else:
        tabs = _one_stage_tables(n)
        kf = _hy_spectrum1(kfilt, tabs)
    cw = conv_w.astype(F32)
    cbias = conv_b.astype(F32).reshape(1, -1)
    z = _hy_long_conv(big, BLK_HY, True, big, BLK_HY + 1, cw, cbias, bias, kf, 0, tabs, two_stage)
    return _hy_long_conv(z, 0, False, big, BLK_HY + 2, cw, cbias, bias, kf, 1, tabs, two_stage)


def _rope_tables(n):
    rows = n // GRID_W
    row = jnp.repeat(jnp.arange(rows), GRID_W)
    col = jnp.tile(jnp.arange(GRID_W), rows)
    quarter = DA_HEAD_DIM // 4
    inv = ROPE_BASE ** (-jnp.arange(quarter, dtype=F32) / quarter)
    ang = jnp.stack([row, col], axis=-1).astype(F32)[..., None] * inv
    cos, sin = jnp.cos(ang), jnp.sin(ang)
    cos_h = jnp.concatenate([cos, cos], axis=-1).reshape(n, DA_HEAD_DIM)
    sin_h = jnp.concatenate([-sin, sin], axis=-1).reshape(n, DA_HEAD_DIM)
    return jnp.tile(cos_h, (1, 2)), jnp.tile(sin_h, (1, 2))


def _rope_body(x_ref, c_ref, s_ref, o_ref):
    quarter = DA_HEAD_DIM // 4
    cos, sin = c_ref[...], s_ref[...]
    lane = lax.broadcasted_iota(jnp.int32, cos.shape, 1)
    first_half = (lane % (2 * quarter)) < quarter
    for k in range(x_ref.shape[2] // LANES):
        x = x_ref[0, :, k * LANES:(k + 1) * LANES].astype(F32)
        partner = jnp.where(first_half, pltpu.roll(x, LANES - quarter, 1), pltpu.roll(x, quarter, 1))
        o_ref[0, :, k * LANES:(k + 1) * LANES] = (x * cos + partner * sin).astype(o_ref.dtype)


def _rope_qk(big, cos_t, sin_t):
    b, l, _ = big.shape
    tm = min(l, 512)
    return pl.pallas_call(
        _rope_body,
        grid=(b, l // tm, 2),
        in_specs=[pl.BlockSpec((1, tm, DA_WIDTH), lambda i, m, j: (i, m, BLK_Q + j)),
                  pl.BlockSpec((tm, LANES), lambda i, m, j: (m, 0)),
                  pl.BlockSpec((tm, LANES), lambda i, m, j: (m, 0))],
        out_specs=pl.BlockSpec((1, tm, DA_WIDTH), lambda i, m, j: (i, m, j)),
        out_shape=jax.ShapeDtypeStruct((b, l, 2 * DA_WIDTH), BF16),
        compiler_params=_cparams("parallel", "parallel", "parallel"),
        name="rope_qk",
    )(big, cos_t, sin_t)


def _attn_body(*refs, nseg, lam_init):
    lp_ref, g_ref, q_ref = refs[:3]
    k_refs = refs[3:3 + nseg]
    v_refs = refs[3 + nseg:3 + 2 * nseg]
    o_ref = refs[3 + 2 * nseg]
    lp = lp_ref[...]
    lam = (jnp.exp(jnp.sum(lp[0:1] * lp[1:2], axis=1, keepdims=True))
           - jnp.exp(jnp.sum(lp[2:3] * lp[3:4], axis=1, keepdims=True)) + lam_init)
    scale = DA_HEAD_DIM ** -0.5 * math.log2(math.e)
    tq = q_ref.shape[1]
    sub = min(tq, ATT_SUB)
    lane = lax.broadcasted_iota(jnp.int32, (sub, 2 * DA_HEAD_DIM), 1)
    tiles = [pl.ds(r, sub) for r in range(0, tq, sub)]
    scores = []
    for rows in tiles:
        q = (q_ref[0, rows, :].astype(F32) * scale).astype(BF16)
        per_map = []
        for mp in range(2):
            qm = jnp.where((lane // DA_HEAD_DIM) == mp, q, jnp.zeros_like(q))
            per_map.append([lax.dot_general(qm, k_ref[0], (((1,), (1,)), ((), ())),
                                            preferred_element_type=F32) for k_ref in k_refs])
        scores.append(per_map)
    weights = []
    for per_map in scores:
        probs = []
        for s in per_map:
            mx = functools.reduce(jnp.maximum, [jnp.max(t, axis=1, keepdims=True) for t in s])
            e = [jnp.exp2(t - mx) for t in s]
            tot = functools.reduce(jnp.add, [jnp.sum(t, axis=1, keepdims=True) for t in e])
            probs.append((e, 1.0 / tot))
        (e0, r0), (e1, r1) = probs
        r1 = r1 * lam
        weights.append([(t0 * r0 - t1 * r1).astype(BF16) for t0, t1 in zip(e0, e1)])
    for rows, a in zip(tiles, weights):
        o = None
        for t, v_ref in zip(a, v_refs):
            part = jnp.dot(t, v_ref[0], preferred_element_type=F32)
            o = part if o is None else o + part
        o_ref[0, rows, :] = (_rms(o, g_ref[...]) * (1.0 - lam_init)).astype(o_ref.dtype)


def _diff_attention(q_arr, q_blk0, kv_segs, lam_p, norm_g, lam_init):
    b, lq, _ = q_arr.shape
    tq = min(lq, ATT_TQ)
    nseg = len(kv_segs)
    hw = 2 * DA_HEAD_DIM
    specs = [pl.BlockSpec((4, DA_HEAD_DIM), lambda i, h, m: (0, 0)),
             pl.BlockSpec((1, hw), lambda i, h, m: (0, 0)),
             pl.BlockSpec((1, tq, hw), lambda i, h, m: (i, m, q_blk0 + h))]
    args = [lam_p.astype(F32), norm_g.reshape(1, hw), q_arr]
    for k_arr, k_blk0, _, _ in kv_segs:
        specs.append(pl.BlockSpec((1, k_arr.shape[1], hw), functools.partial(
            lambda i, h, m, o: (i, 0, o + h), o=k_blk0)))
        args.append(k_arr)
    for _, _, v_arr, v_blk0 in kv_segs:
        specs.append(pl.BlockSpec((1, v_arr.shape[1], hw), functools.partial(
            lambda i, h, m, o: (i, 0, o + h), o=v_blk0)))
        args.append(v_arr)
    return pl.pallas_call(
        functools.partial(_attn_body, nseg=nseg, lam_init=lam_init),
        grid=(b, DA_HEADS, lq // tq),
        in_specs=specs,
        out_specs=pl.BlockSpec((1, tq, hw), lambda i, h, m: (i, m, h)),
        out_shape=jax.ShapeDtypeStruct((b, lq, DA_WIDTH), BF16),
        compiler_params=_cparams("parallel", "parallel", "arbitrary"),
        name="diff_attention",
    )(*args)


def _merge_body(s_ref, h_ref, a_ref, g0_ref, g1_ref, g2_ref, wb_ref, wo_ref, x_ref, m_ref, ng_ref, o_ref):
    mixed = None
    for br, gl, i in ((s_ref, g0_ref, 0), (h_ref, g1_ref, 1), (a_ref, g2_ref, 2)):
        t = jax.nn.sigmoid(gl[0].astype(F32)) * jnp.dot(br[0], wb_ref[i], preferred_element_type=F32)
        mixed = t if mixed is None else mixed + t
    out = jnp.dot(mixed.astype(BF16), wo_ref[...], preferred_element_type=F32)
    o_ref[0] = x_ref[0] + m_ref[0] * _rms(out, ng_ref[...])


def _merge_residual(ssd, hy, da, big, w_branch, w_out, x, mod_gate, norm_g):
    b, l, d = x.shape
    tm = min(l, 512)
    row = lambda col: pl.BlockSpec((1, tm, d), lambda i, m: (i, m, col))
    return pl.pallas_call(
        _merge_body,
        grid=(b, l // tm),
        in_specs=[row(0), row(0), row(0), row(BLK_GATE), row(BLK_GATE + 1), row(BLK_GATE + 2),
                  pl.BlockSpec((3, d, d), lambda i, m: (0, 0, 0)),
                  pl.BlockSpec((d, d), lambda i, m: (0, 0)),
                  row(0),
                  pl.BlockSpec((1, 1, d), lambda i, m: (i, 0, 0)),
                  pl.BlockSpec((1, d), lambda i, m: (0, 0))],
        out_specs=row(0),
        out_shape=jax.ShapeDtypeStruct((b, l, d), F32),
        compiler_params=_cparams("parallel", "parallel"),
        name="merge_residual",
    )(ssd, hy, da, big, big, big, w_branch, w_out, x, mod_gate.reshape(b, 1, d), norm_g.reshape(1, d))


def _ffn_down_body(gt_ref, p_ref, n_ref, up_ref, cw_ref, cb_ref, wd_ref, x_ref, m_ref, ng_ref, o_ref):
    m, nm = pl.program_id(1), pl.num_programs(1)
    gate = _conv_rows(gt_ref[0].astype(F32), p_ref[0].astype(F32), n_ref[0].astype(F32),
                      cw_ref[...], cb_ref[...], m, nm)
    act = (_silu(gate) * up_ref[0].astype(F32)).astype(BF16)
    out = jnp.dot(act, wd_ref[...], preferred_element_type=F32)
    o_ref[0] = x_ref[0] + m_ref[0] * _rms(out, ng_ref[...])


def _ffn_down_residual(up, conv_w, conv_b, w_down, x, mod_gate, norm_g):
    b, l, d = x.shape
    f = D_FF
    tm = min(l, 512)
    prev, nxt = _halo_specs(tm, l, f, 0)
    return pl.pallas_call(
        _ffn_down_body,
        grid=(b, l // tm),
        in_specs=[pl.BlockSpec((1, tm, f), lambda i, m: (i, m, 0)), prev, nxt,
                  pl.BlockSpec((1, tm, f), lambda i, m: (i, m, 1)),
                  pl.BlockSpec((FFN_CONV, f), lambda i, m: (0, 0)),
                  pl.BlockSpec((1, f), lambda i, m: (0, 0)),
                  pl.BlockSpec((f, d), lambda i, m: (0, 0)),
                  pl.BlockSpec((1, tm, d), lambda i, m: (i, m, 0)),
                  pl.BlockSpec((1, 1, d), lambda i, m: (i, 0, 0)),
                  pl.BlockSpec((1, d), lambda i, m: (0, 0))],
        out_specs=pl.BlockSpec((1, tm, d), lambda i, m: (i, m, 0)),
        out_shape=jax.ShapeDtypeStruct((b, l, d), F32),
        compiler_params=_cparams("parallel", "parallel"),
        name="ffn_down_residual",
    )(up, up, up, up, conv_w.astype(F32), conv_b.astype(F32).reshape(1, f), w_down, x,
      mod_gate.reshape(b, 1, d), norm_g.reshape(1, d))


def _project(x, g, shift, scale, w_big, w_xbc, w_dt):
    big = _norm_mod_matmul(x, g, shift, scale, w_big, PROJ_TM, 1024, BF16, "proj_big")
    xbc = _norm_mod_matmul(x, g, shift, scale, w_xbc, PROJ_TM, 512, BF16, "proj_xbc")
    dt = _norm_mod_matmul(x, g, shift, scale, w_dt, PROJ_TM, LANES, F32, "proj_dt")
    return big, xbc, dt


def _lane_row(v):
    flat = v.astype(F32).reshape(1, 2 * SSD_HEADS)
    return jnp.pad(flat, ((0, 0), (0, LANES - 2 * SSD_HEADS)))


def kernel(x, c, ctx, c_ctx, w_ada, b_ada, norm_g, w_in, ssd_conv_w, ssd_conv_b, ssd_a_log, ssd_dt_bias, ssd_d, ssd_norm, hy_conv_w, hy_conv_b, hy_w1, hy_b1, hy_f1, hy_w2, hy_b2, hy_f2, hy_w3, hy_bias, da_lambda, da_norm, w_branch, w_out, ffn_w_up, ffn_conv_w, ffn_conv_b, ffn_w_down):
    b, seq, d = x.shape
    depth = w_ada.shape[0]
    cos_t, sin_t = _rope_tables(seq)
    mod = _ada_mod(c, c_ctx, w_ada, b_ada)
    x_l, x_c = x, ctx
    hshape = (b, SSD_GROUPS, SSD_STATE, SSD_GROUP_W)
    for i in range(depth):
        ctx_out = i < depth - 1
        ml = [mod[i, :b, k * d:(k + 1) * d] for k in range(6)]
        mc = [jnp.broadcast_to(mod[i, b:b + 1, k * d:(k + 1) * d], (b, d)) for k in range(6)]
        ng = norm_g[i]
        wi = w_in[i]
        w_big = jnp.concatenate([wi[:, :SSD_INNER], wi[:, OFF_HY:]], axis=1).astype(BF16)
        w_xbc = wi[:, SSD_INNER:SSD_INNER + SSD_CONV_CH].astype(BF16)
        w_dt = jnp.pad(wi[:, SSD_INNER + SSD_CONV_CH:IN_SSD], ((0, 0), (0, LANES - 2 * SSD_HEADS))).astype(BF16)
        big_c, xbc_c, dt_c = _project(x_c, ng[0], mc[0], mc[1], w_big, w_xbc, w_dt)
        big_l, xbc_l, dt_l = _project(x_l, ng[0], ml[0], ml[1], w_big, w_xbc, w_dt)

        conv_w = ssd_conv_w[i].astype(F32)
        dtb, alog = _lane_row(ssd_dt_bias[i]), _lane_row(ssd_a_log[i])
        dskip_x = jnp.repeat(ssd_d[i].astype(F32), SSD_HEAD_DIM).reshape(1, SSD_INNER)
        h0 = jnp.zeros(hshape, F32)
        ssd_c, hf, hb = _ssd_branch(_ssd_prep(xbc_c, conv_w, ssd_conv_b[i].astype(F32)), dt_c, big_c,
                                    dtb, alog, dskip_x, ssd_norm[i], h0, h0)
        ssd_l, _, _ = _ssd_branch(_ssd_prep(xbc_l, conv_w, ssd_conv_b[i].astype(F32)), dt_l, big_l,
                                  dtb, alog, dskip_x, ssd_norm[i], hf, hb)

        filt = (hy_w1[i], hy_b1[i], hy_f1[i], hy_w2[i], hy_b2[i], hy_f2[i], hy_w3[i])
        hy_l = _hyena_branch(big_l, hy_conv_w[i], hy_conv_b[i], hy_bias[i].astype(F32), filt, True)

        lam_init = 0.8 - 0.6 * math.exp(-0.3 * i)
        qk_l = _rope_qk(big_l, cos_t, sin_t)
        hpb = D_MODEL // (2 * DA_HEAD_DIM)
        segs = [(big_c, BLK_K * hpb, big_c, BLK_V * hpb), (qk_l, hpb, big_l, BLK_V * hpb)]
        da_l = _diff_attention(qk_l, 0, segs, da_lambda[i], da_norm[i], lam_init)

        wb = w_branch[i].astype(BF16)
        wo = w_out[i].astype(BF16)
        w_up = ffn_w_up[i].astype(BF16)
        w_dn = ffn_w_down[i].astype(BF16)
        x_l = _merge_residual(ssd_l, hy_l, da_l, big_l, wb, wo, x_l, ml[2], ng[1])
        up_l = _norm_mod_matmul(x_l, ng[2], ml[3], ml[4], w_up, FFN_TM, 1408, BF16, "ffn_up")
        x_l = _ffn_down_residual(up_l, ffn_conv_w[i], ffn_conv_b[i], w_dn, x_l, ml[5], ng[3])
        if ctx_out:
            hy_c = _hyena_branch(big_c, hy_conv_w[i], hy_conv_b[i], hy_bias[i].astype(F32), filt, False)
            da_c = _diff_attention(big_c, BLK_Q * hpb, [(big_c, BLK_K * hpb, big_c, BLK_V * hpb)],
                                   da_lambda[i], da_norm[i], lam_init)
            x_c = _merge_residual(ssd_c, hy_c, da_c, big_c, wb, wo, x_c, mc[2], ng[1])
            up_c = _norm_mod_matmul(x_c, ng[2], mc[3], mc[4], w_up, FFN_TM, 1408, BF16, "ffn_up")
            x_c = _ffn_down_residual(up_c, ffn_conv_w[i], ffn_conv_b[i], w_dn, x_c, mc[5], ng[3])
    return x_l
```

```python
import functools
import math

import numpy as np
import jax
import jax.numpy as jnp
from jax import lax
from jax.experimental import pallas as pl
from jax.experimental.pallas import tpu as pltpu

F32 = jnp.float32
BF16 = jnp.bfloat16
HIGHEST = lax.Precision.HIGHEST

D_MODEL = 1024
GRID_W = 64
EPS = 1e-6

SSD_INNER = D_MODEL
SSD_HEAD_DIM = 64
SSD_HEADS = SSD_INNER // SSD_HEAD_DIM
SSD_GROUPS = 2
SSD_STATE = 128
SSD_CONV = 5
SSD_CHUNK = 128
SSD_CONV_CH = SSD_INNER + 2 * SSD_GROUPS * SSD_STATE
SSD_GROUP_W = SSD_INNER // SSD_GROUPS

HY_WIDTH = D_MODEL
HY_ORDER = 2
HY_SHORT = 3
HY_BANDS = 16
HY_EMB = 1 + 2 * HY_BANDS
HY_FF = 64
HY_TARGET = 1e-2
HY_FAST = 0.3
HY_SLOW = 1.5
HY_N2 = 64
HY_CB = 128
HY_CB1 = 512
HY_PAD = 8
HY_UNROLL = 16

DA_HEAD_DIM = 64
DA_HEADS = D_MODEL // (2 * DA_HEAD_DIM)
DA_WIDTH = DA_HEADS * 2 * DA_HEAD_DIM
ROPE_BASE = 10000.0
ATT_TQ = 512
ATT_SUB = 256
D_FF = ((8 * D_MODEL // 3 + 127) // 128) * 128
FFN_CONV = 3

IN_SSD = SSD_INNER + SSD_CONV_CH + 2 * SSD_HEADS
IN_HY = (HY_ORDER + 1) * HY_WIDTH
IN_DA = 3 * DA_WIDTH
OFF_HY = IN_SSD
OFF_DA = OFF_HY + IN_HY
OFF_GATE = OFF_DA + IN_DA

BLK_Z, BLK_HY, BLK_Q, BLK_K, BLK_V, BLK_GATE = 0, 1, 4, 5, 6, 7
BIG_COLS = 10 * D_MODEL

VMEM_LIMIT = 56 * 1024 * 1024
PROJ_TM = 2048
FFN_TM = 1024
LANES = 128
HALO = 16


def _cparams(*sem):
    return pltpu.CompilerParams(dimension_semantics=sem, vmem_limit_bytes=VMEM_LIMIT)


def _silu(x):
    return x * jax.nn.sigmoid(x)


def _rms(x, g):
    return x * lax.rsqrt(jnp.mean(x * x, axis=-1, keepdims=True) + EPS) * g


def _round_up(a, m):
    return (a + m - 1) // m * m


def _pitch(rows):
    p8 = _round_up(rows, 8) // 8
    return 8 * (p8 if p8 % 2 else p8 + 1)


def _ada_body(s_ref, w_ref, b_ref, o_ref):
    s = _silu(s_ref[...])
    o_ref[0] = jnp.dot(s, w_ref[0], preferred_element_type=F32, precision=HIGHEST) + b_ref[0]


def _ada_mod(c, c_ctx, w_ada, b_ada):
    depth, d, n6 = w_ada.shape
    b = c.shape[0]
    rows = _round_up(b + 1, 8)
    s = jnp.zeros((rows, d), F32).at[:b].set(c).at[b].set(c_ctx)
    tn = 1536
    return pl.pallas_call(
        _ada_body,
        grid=(depth, n6 // tn),
        in_specs=[pl.BlockSpec((rows, d), lambda l, j: (0, 0)),
                  pl.BlockSpec((1, d, tn), lambda l, j: (l, 0, j)),
                  pl.BlockSpec((1, 1, tn), lambda l, j: (l, 0, j))],
        out_specs=pl.BlockSpec((1, rows, tn), lambda l, j: (l, 0, j)),
        out_shape=jax.ShapeDtypeStruct((depth, rows, n6), F32),
        compiler_params=_cparams("parallel", "parallel"),
        name="ada_mod",
    )(s, w_ada, b_ada.reshape(depth, 1, n6))


def _nmm_body(x_ref, g_ref, sh_ref, sc_ref, w_ref, o_ref, h_scr):
    @pl.when(pl.program_id(2) == 0)
    def _():
        h = _rms(x_ref[0], g_ref[...]) * (1.0 + sc_ref[0]) + sh_ref[0]
        h_scr[...] = h.astype(BF16)

    o_ref[0] = jnp.dot(h_scr[...], w_ref[...], preferred_element_type=F32).astype(o_ref.dtype)


def _rotate_heads(x, cos, sin):
    quarter = DA_HEAD_DIM // 4
    lane = lax.broadcasted_iota(jnp.int32, cos.shape, 1)
    first_half = (lane % (2 * quarter)) < quarter
    out = []
    for k in range(x.shape[1] // LANES):
        xk = x[:, k * LANES:(k + 1) * LANES]
        xb = xk.astype(BF16)
        partner = jnp.where(first_half, pltpu.roll(xb, LANES - quarter, 1), pltpu.roll(xb, quarter, 1))
        out.append(xk * cos + partner.astype(F32) * sin)
    return jnp.concatenate(out, axis=1)


def _nmm_rope_body(x_ref, g_ref, sh_ref, sc_ref, w_ref, c_ref, s_ref, o_ref, h_scr, *, rope_blocks):
    j = pl.program_id(2)

    @pl.when(j == 0)
    def _():
        h = _rms(x_ref[0], g_ref[...]) * (1.0 + sc_ref[0]) + sh_ref[0]
        h_scr[...] = h.astype(BF16)

    acc = jnp.dot(h_scr[...], w_ref[...], preferred_element_type=F32)
    rotated = functools.reduce(jnp.logical_or, [j == blk for blk in rope_blocks])

    @pl.when(rotated)
    def _():
        o_ref[0] = _rotate_heads(acc, c_ref[...], s_ref[...]).astype(o_ref.dtype)

    @pl.when(jnp.logical_not(rotated))
    def _():
        o_ref[0] = acc.astype(o_ref.dtype)


def _norm_mod_matmul(x, g, shift, scale, w, tm, tn, out_dtype, name, rope=None):
    b, l, d = x.shape
    n = w.shape[1]
    tm = min(l, tm)
    specs = [pl.BlockSpec((1, tm, d), lambda i, m, j: (i, m, 0)),
             pl.BlockSpec((1, d), lambda i, m, j: (0, 0)),
             pl.BlockSpec((1, 1, d), lambda i, m, j: (i, 0, 0)),
             pl.BlockSpec((1, 1, d), lambda i, m, j: (i, 0, 0)),
             pl.BlockSpec((d, tn), lambda i, m, j: (0, j))]
    args = [x, g.reshape(1, d), shift.reshape(b, 1, d), scale.reshape(b, 1, d), w]
    body = _nmm_body
    if rope is not None:
        cos_t, sin_t, blocks = rope
        specs += [pl.BlockSpec((tm, LANES), lambda i, m, j: (m, 0))] * 2
        args += [cos_t, sin_t]
        body = functools.partial(_nmm_rope_body, rope_blocks=blocks)
    return pl.pallas_call(
        body,
        grid=(b, l // tm, n // tn),
        in_specs=specs,
        out_specs=pl.BlockSpec((1, tm, tn), lambda i, m, j: (i, m, j)),
        out_shape=jax.ShapeDtypeStruct((b, l, n), out_dtype),
        scratch_shapes=[pltpu.VMEM((tm, d), BF16)],
        compiler_params=_cparams("parallel", "parallel", "arbitrary"),
        name=name,
    )(*args)


def _halo_specs(tl, l, c, col):
    per = tl // HALO
    last = l // HALO - 1
    prev = pl.BlockSpec((1, HALO, c), lambda i, m: (i, jnp.maximum(m * per - 1, 0), col))
    nxt = pl.BlockSpec((1, HALO, c), lambda i, m: (i, jnp.minimum((m + 1) * per, last), col))
    return prev, nxt


def _conv_rows(x, prev, nxt, w, bias, m, nm):
    tl = x.shape[0]
    k = w.shape[0]
    pad = k // 2
    prev = prev * (m > 0).astype(F32)
    nxt = nxt * (m < nm - 1).astype(F32)
    ext = jnp.concatenate([prev, x, nxt], axis=0)
    rows = tl + 2 * HALO
    y = bias
    for j in range(k):
        shifted = ext if j == pad else pltpu.roll(ext, (pad - j) % rows, 0)
        y = y + shifted[HALO:HALO + tl] * w[j:j + 1]
    return y


def _ssd_prep_body(x_ref, p_ref, n_ref, w_ref, b_ref, o_ref):
    m, nm = pl.program_id(1), pl.num_programs(1)
    y = _conv_rows(x_ref[0].astype(F32), p_ref[0].astype(F32), n_ref[0].astype(F32),
                   w_ref[...], b_ref[...], m, nm)
    o_ref[0] = _silu(y).astype(o_ref.dtype)


def _ssd_prep(xbc_raw, conv_w, conv_b):
    b, l, c = xbc_raw.shape
    tl = min(l, 512)
    prev, nxt = _halo_specs(tl, l, c, 0)
    return pl.pallas_call(
        _ssd_prep_body,
        grid=(b, l // tl),
        in_specs=[pl.BlockSpec((1, tl, c), lambda i, m: (i, m, 0)), prev, nxt,
                  pl.BlockSpec((SSD_CONV, c), lambda i, m: (0, 0)),
                  pl.BlockSpec((1, c), lambda i, m: (0, 0))],
        out_specs=pl.BlockSpec((1, tl, c), lambda i, m: (i, m, 0)),
        out_shape=jax.ShapeDtypeStruct((b, l, c), BF16),
        compiler_params=_cparams("parallel", "parallel"),
        name="ssd_prep",
    )(xbc_raw, xbc_raw, xbc_raw, conv_w, conv_b.reshape(1, c))


def _softplus(x):
    return jnp.maximum(x, 0.0) + jnp.log1p(jnp.exp(-jnp.abs(x)))


def _ssd_chunk(xbc, dt_raw, dtb, alog, h_scr, reverse, col0):
    cs = xbc.shape[0]
    gw = SSD_GROUPS * SSD_STATE
    x = xbc[:, :SSD_INNER].astype(F32)
    dt = _softplus(dt_raw + dtb)
    da = dt * (-jnp.exp(alog))
    row = lax.broadcasted_iota(jnp.int32, (cs, cs), 0)
    col = lax.broadcasted_iota(jnp.int32, (cs, cs), 1)
    keep = (row <= col) if reverse else (row >= col)
    acum = jnp.dot(keep.astype(F32), da, preferred_element_type=F32, precision=HIGHEST)
    acum_t = acum.T
    head_of_lane = col0 + lax.broadcasted_iota(jnp.int32, (2 * LANES, SSD_INNER), 1) // SSD_HEAD_DIM
    src_lane = lax.broadcasted_iota(jnp.int32, (2 * LANES, SSD_INNER), 0) % LANES
    expand = (src_lane == head_of_lane).astype(BF16)

    def per_head_lanes(v):
        hi = v.astype(BF16)
        lo = (v - hi.astype(F32)).astype(BF16)
        return jnp.dot(jnp.concatenate([hi, lo], axis=1), expand, preferred_element_type=F32)

    end = 0 if reverse else cs - 1
    into = per_head_lanes(jnp.exp(acum))
    carry_w = per_head_lanes(jnp.exp(acum[end:end + 1] - acum))
    dec = into[end:end + 1]
    xdt = x * per_head_lanes(dt)
    xw = (xdt * carry_w).astype(BF16)
    xdt_b = xdt.astype(BF16)
    lane = lax.broadcasted_iota(jnp.int32, (cs, LANES), 1)
    parts = []
    for g in range(SSD_GROUPS):
        bm = xbc[:, SSD_INNER + g * SSD_STATE:SSD_INNER + (g + 1) * SSD_STATE]
        cm = xbc[:, SSD_INNER + gw + g * SSD_STATE:SSD_INNER + gw + (g + 1) * SSD_STATE]
        cb = lax.dot_general(cm, bm, (((1,), (1,)), ((), ())), preferred_element_type=F32)
        sl = slice(g * SSD_GROUP_W, (g + 1) * SSD_GROUP_W)
        h_t = h_scr[g]
        y_off = jnp.dot(cm, h_t.astype(BF16), preferred_element_type=F32) * into[:, sl]
        bm_t = bm.astype(F32).T.astype(BF16)
        h_scr[g] = h_t * dec[:, sl] + jnp.dot(bm_t, xw[:, sl], preferred_element_type=F32)
        for p in range(SSD_GROUP_W // LANES):
            lo = g * SSD_GROUP_W + p * LANES
            xp = xdt_b[:, lo:lo + LANES]
            ys = []
            for q in range(LANES // SSD_HEAD_DIM):
                hc = col0 + lo // SSD_HEAD_DIM + q
                seg = acum[:, hc:hc + 1] - acum_t[hc:hc + 1, :]
                dmat = (cb * jnp.exp(jnp.where(keep, seg, -1e30))).astype(BF16)
                ys.append(jnp.dot(dmat, xp, preferred_element_type=F32))
            y_diag = jnp.where(lane < SSD_HEAD_DIM, ys[0], ys[1])
            parts.append(y_diag + y_off[:, p * LANES:(p + 1) * LANES])
    return jnp.concatenate(parts, axis=1), x


def _ssd_fwd_body(xbc_ref, dt_ref, dtb_ref, alog_ref, h0_ref, y_ref, hl_ref, h_scr):
    c = pl.program_id(1)

    @pl.when(c == 0)
    def _():
        h_scr[...] = h0_ref[0]

    y, _ = _ssd_chunk(xbc_ref[0], dt_ref[0], dtb_ref[...], alog_ref[...], h_scr, False, 0)
    y_ref[0] = y

    @pl.when(c == pl.num_programs(1) - 1)
    def _():
        hl_ref[0] = h_scr[...]


def _ssd_bwd_body(xbc_ref, dt_ref, dtb_ref, alog_ref, h0_ref, yf_ref, z_ref, dsk_ref, ng_ref,
                  o_ref, hl_ref, h_scr):
    c = pl.program_id(1)

    @pl.when(c == 0)
    def _():
        h_scr[...] = h0_ref[0]

    yb, x = _ssd_chunk(xbc_ref[0], dt_ref[0], dtb_ref[...], alog_ref[...], h_scr, True, SSD_HEADS)
    y = yf_ref[0] + yb + x * dsk_ref[...]
    o_ref[0] = _rms(y * _silu(z_ref[0].astype(F32)), ng_ref[...]).astype(o_ref.dtype)

    @pl.when(c == pl.num_programs(1) - 1)
    def _():
        hl_ref[0] = h_scr[...]


def _ssd_branch(xbc_act, dt_raw, big, dtb, alog, dskip_x, norm_g, h0f, h0b):
    b, l, _ = xbc_act.shape
    cs = SSD_CHUNK
    nc = l // cs
    hshape = (SSD_GROUPS, SSD_STATE, SSD_GROUP_W)
    state_spec = pl.BlockSpec((1,) + hshape, lambda i, c: (i, 0, 0, 0))
    vec = lambda w: pl.BlockSpec((1, w), lambda i, c: (0, 0))
    state_shape = jax.ShapeDtypeStruct((b,) + hshape, F32)
    yf, hf = pl.pallas_call(
        _ssd_fwd_body,
        grid=(b, nc),
        in_specs=[pl.BlockSpec((1, cs, SSD_CONV_CH), lambda i, c: (i, c, 0)),
                  pl.BlockSpec((1, cs, LANES), lambda i, c: (i, c, 0)),
                  vec(LANES), vec(LANES), state_spec],
        out_specs=[pl.BlockSpec((1, cs, SSD_INNER), lambda i, c: (i, c, 0)), state_spec],
        out_shape=[jax.ShapeDtypeStruct((b, l, SSD_INNER), F32), state_shape],
        scratch_shapes=[pltpu.VMEM(hshape, F32)],
        compiler_params=_cparams("parallel", "arbitrary"),
        name="ssd_scan_fwd",
    )(xbc_act, dt_raw, dtb, alog, h0f)
    rev = lambda i, c: (i, nc - 1 - c, 0)
    out, hb = pl.pallas_call(
        _ssd_bwd_body,
        grid=(b, nc),
        in_specs=[pl.BlockSpec((1, cs, SSD_CONV_CH), rev),
                  pl.BlockSpec((1, cs, LANES), rev),
                  vec(LANES), vec(LANES), state_spec,
                  pl.BlockSpec((1, cs, SSD_INNER), rev),
                  pl.BlockSpec((1, cs, SSD_INNER), lambda i, c: (i, nc - 1 - c, BLK_Z)),
                  vec(SSD_INNER), vec(SSD_INNER)],
        out_specs=[pl.BlockSpec((1, cs, SSD_INNER), rev), state_spec],
        out_shape=[jax.ShapeDtypeStruct((b, l, SSD_INNER), BF16), state_shape],
        scratch_shapes=[pltpu.VMEM(hshape, F32)],
        compiler_params=_cparams("parallel", "arbitrary"),
        name="ssd_scan_bwd",
    )(xbc_act, dt_raw, dtb, alog, h0b, yf, big, dskip_x, norm_g.reshape(1, SSD_INNER))
    return out, hf, hb


def _hy_filter_body(z_ref, w1_ref, b1_ref, f1_ref, w2_ref, b2_ref, f2_ref, w3_ref, dl_ref, o_ref, *, n, tr):
    z = z_ref[...]
    h = jnp.sin(f1_ref[...] * (jnp.dot(z, w1_ref[...], preferred_element_type=F32, precision=HIGHEST)
                               + b1_ref[...]))
    h = jnp.sin(f2_ref[...] * (jnp.dot(h, w2_ref[...], preferred_element_type=F32, precision=HIGHEST)
                               + b2_ref[...]))
    filt = jnp.dot(h.astype(BF16), w3_ref[...], preferred_element_type=F32)
    filt = filt * jnp.exp(-z[:, 0:1] * dl_ref[...])
    rows = pl.program_id(0) * tr + lax.broadcasted_iota(jnp.int32, (tr, 1), 0)
    o_ref[...] = jnp.where(rows == n, 0.0, filt)


def _hy_features(n):
    t = jnp.linspace(0.0, 1.0, n, dtype=F32)[:, None]
    w = (2.0 * math.pi / n) * jnp.arange(n, dtype=F32)[:, None]
    bands = jnp.linspace(1e-4, HY_BANDS - 1, HY_BANDS, dtype=F32)
    z = jnp.concatenate([t, jnp.cos(bands * w), -jnp.sin(bands * w)], axis=-1)
    return jnp.pad(z, ((0, 0), (0, LANES - HY_EMB)))


def _hy_filter(n, w1, b1, f1, w2, b2, f2, w3):
    z = _hy_features(n)
    zk = jnp.concatenate([z, z[:1], jnp.flip(z[1:], axis=0)], axis=0)
    padw = lambda a, r, c: jnp.pad(a.astype(F32), ((0, r - a.shape[0]), (0, c - a.shape[1])))
    w1p = padw(w1, LANES, LANES)
    w2p = padw(w2, LANES, LANES)
    w3p = padw(w3, LANES, w3.shape[1]).astype(BF16)
    rowp = lambda a: padw(a.reshape(1, -1), 1, LANES)
    deltas = jnp.abs(jnp.linspace(math.log(HY_TARGET) / HY_SLOW, math.log(HY_TARGET) / HY_FAST,
                                  HY_WIDTH, dtype=F32))
    dl = jnp.tile(deltas, HY_ORDER).reshape(1, HY_ORDER * HY_WIDTH)
    tr = min(n, 512)
    wide = HY_ORDER * HY_WIDTH
    full = lambda r, c: pl.BlockSpec((r, c), lambda i: (0, 0))
    return pl.pallas_call(
        functools.partial(_hy_filter_body, n=n, tr=tr),
        grid=(2 * n // tr,),
        in_specs=[pl.BlockSpec((tr, LANES), lambda i: (i, 0)),
                  full(LANES, LANES), full(1, LANES), full(1, LANES),
                  full(LANES, LANES), full(1, LANES), full(1, LANES),
                  pl.BlockSpec((LANES, wide), lambda i: (0, (i * tr) // n)),
                  full(1, wide)],
        out_specs=pl.BlockSpec((tr, wide), lambda i: (i, 0)),
        out_shape=jax.ShapeDtypeStruct((2 * n, wide), F32),
        compiler_params=_cparams("parallel"),
        name="hy_filter",
    )(zk, w1p, rowp(b1), rowp(f1), w2p, rowp(b2), rowp(f2), w3p, dl)


def _two_stage_tables(n, filt_rows):
    big_n = 2 * n
    n2 = HY_N2
    n1 = big_n // n2
    k1n = n1 // 2 + 1
    p = _round_up(k1n, 8)
    s2 = np.arange(n2)
    k1 = np.arange(k1n)

    def first(s_rows):
        s1 = np.arange(s_rows)
        ang = 2.0 * np.pi * ((k1[None, :, None] * (n2 * s1[None, None, :] + s2[:, None, None])) % big_n) / big_n
        out = np.zeros((n2, 2 * p, s_rows), np.float32)
        out[:, :k1n] = np.cos(ang)
        out[:, p:p + k1n] = -np.sin(ang)
        return out

    ang2 = 2.0 * np.pi * ((s2[:, None] * s2[None, :]) % n2) / n2
    c2, sn2 = np.cos(ang2), np.sin(ang2)
    second = np.block([[c2, sn2], [-sn2, c2]]).astype(np.float32)
    second_inv = np.block([[c2, -sn2], [sn2, c2]]).astype(np.float32)
    s1o = np.arange(n // n2)
    wgt = np.where((k1 == 0) | (k1 == n1 // 2), 1.0, 2.0) / big_n
    ang = 2.0 * np.pi * ((k1[None, None, :] * (n2 * s1o[None, :, None] + s2[:, None, None])) % big_n) / big_n
    last = np.zeros((n2, n // n2, 2 * p), np.float32)
    last[:, :, :k1n] = wgt * np.cos(ang)
    last[:, :, p:p + k1n] = -wgt * np.sin(ang)
    cast = lambda a: jnp.asarray(a).astype(BF16)
    return dict(k1n=k1n, p=p, first=cast(first(n // n2)), first_filt=cast(first(filt_rows // n2)),
                second=cast(second), second_inv=cast(second_inv), last=cast(last))


def _one_stage_tables(n):
    big_n = 2 * n
    kn = n + 1
    p = _round_up(kn, 8)
    k = np.arange(kn)

    def fwd(rows):
        s = np.arange(rows)
        ang = 2.0 * np.pi * ((k[:, None] * s[None, :]) % big_n) / big_n
        out = np.zeros((2 * p, rows), np.float32)
        out[:kn] = np.cos(ang)
        out[p:p + kn] = -np.sin(ang)
        return out

    s = np.arange(n)
    wgt = np.where((k == 0) | (k == n), 1.0, 2.0) / big_n
    ang = 2.0 * np.pi * ((s[:, None] * k[None, :]) % big_n) / big_n
    inv = np.zeros((n, 2 * p), np.float32)
    inv[:, :kn] = wgt * np.cos(ang)
    inv[:, p:p + kn] = -wgt * np.sin(ang)
    cast = lambda a: jnp.asarray(a).astype(BF16)
    return dict(p=p, fwd=cast(fwd(n)), fwd_filt=cast(fwd(big_n)), inv=cast(inv))


def _spec2_body(k_ref, f1_ref, f2_ref, o_ref, k_scr, a_scr, *, k1n, p, s_rows):
    n2 = HY_N2
    ap, gp = _pitch(2 * p), _pitch(n2)
    for g in range(s_rows):
        k_scr[pl.ds(g * gp, n2), :] = k_ref[pl.ds(g * n2, n2), :]

    def stage1(s2, carry):
        rows = k_scr[pl.ds(s2, s_rows, stride=gp), :]
        a = jnp.dot(f1_ref[s2], rows.astype(BF16), preferred_element_type=F32)
        a_scr[pl.ds(pl.multiple_of(s2 * ap, 8), 2 * p), :] = a
        return carry

    lax.fori_loop(0, n2, stage1, 0, unroll=HY_UNROLL)

    def stage2(k1, carry):
        re = a_scr[pl.ds(k1, n2, stride=ap), :]
        im = a_scr[pl.ds(p + k1, n2, stride=ap), :]
        a = jnp.concatenate([re, im], axis=0).astype(BF16)
        o_ref[k1] = jnp.dot(f2_ref[...], a, preferred_element_type=F32)
        return carry

    lax.fori_loop(0, k1n, stage2, 0, unroll=HY_UNROLL)


def _hy_spectrum2(kfilt, tabs):
    rows, wide = kfilt.shape
    k1n, p = tabs["k1n"], tabs["p"]
    s_rows = rows // HY_N2
    cb = HY_CB
    return pl.pallas_call(
        functools.partial(_spec2_body, k1n=k1n, p=p, s_rows=s_rows),
        grid=(wide // cb,),
        in_specs=[pl.BlockSpec((rows, cb), lambda j: (0, j)),
                  pl.BlockSpec((HY_N2, 2 * p, s_rows), lambda j: (0, 0, 0)),
                  pl.BlockSpec((2 * HY_N2, 2 * HY_N2), lambda j: (0, 0))],
        out_specs=pl.BlockSpec((k1n, 2 * HY_N2, cb), lambda j: (0, 0, j)),
        out_shape=jax.ShapeDtypeStruct((k1n, 2 * HY_N2, wide), F32),
        scratch_shapes=[pltpu.VMEM((s_rows * _pitch(HY_N2), cb), F32),
                        pltpu.VMEM((HY_N2 * _pitch(2 * p), cb), F32)],
        compiler_params=_cparams("parallel"),
        name="hy_spectrum2",
    )(kfilt, tabs["first_filt"], tabs["second"])


def _spec1_body(k_ref, f_ref, o_ref):
    o_ref[...] = jnp.dot(f_ref[...], k_ref[...].astype(BF16), preferred_element_type=F32)


def _hy_spectrum1(kfilt, tabs):
    rows, wide = kfilt.shape
    p = tabs["p"]
    cb = HY_CB1
    return pl.pallas_call(
        _spec1_body,
        grid=(wide // cb,),
        in_specs=[pl.BlockSpec((rows, cb), lambda j: (0, j)),
                  pl.BlockSpec((2 * p, rows), lambda j: (0, 0))],
        out_specs=pl.BlockSpec((2 * p, cb), lambda j: (0, j)),
        out_shape=jax.ShapeDtypeStruct((2 * p, wide), F32),
        compiler_params=_cparams("parallel"),
        name="hy_spectrum1",
    )(kfilt, tabs["fwd_filt"])


def _fill_pad(src_ref, pad_scr, n, rb):
    for r in range(0, n, rb):
        pad_scr[pl.ds(HY_PAD + r, rb), :] = src_ref[0, pl.ds(r, rb), :].astype(F32)


def _zero_pad_rows(pad_scr, n):
    zeros = jnp.zeros((HY_PAD, pad_scr.shape[1]), F32)
    pad_scr[pl.ds(0, HY_PAD), :] = zeros
    pad_scr[pl.ds(HY_PAD + n, HY_PAD), :] = zeros


def _conv3_group(pad_scr, w_ref, b_ref, g, grp):
    acc = b_ref[...]
    for j in range(HY_SHORT):
        acc = acc + pad_scr[pl.ds(HY_PAD - 1 + j + g * grp, grp), :] * w_ref[j:j + 1, :]
    return acc


def _load_u(u_ref, cwu_ref, cbu_ref, pad_scr, u_scr, n, rb, conv_u, grp, gp):
    _fill_pad(u_ref, pad_scr, n, rb)
    for g in range(n // grp):
        if conv_u:
            val = _conv3_group(pad_scr, cwu_ref, cbu_ref, g, grp)
        else:
            val = pad_scr[pl.ds(HY_PAD + g * grp, grp), :]
        u_scr[pl.ds(g * gp, grp), :] = val


def _gate_out(g_ref, cwg_ref, cbg_ref, bias_ref, pad_scr, u_scr, y_scr, o_ref, n, rb, grp, gp):
    _fill_pad(g_ref, pad_scr, n, rb)
    for g in range(n // grp):
        gate = _conv3_group(pad_scr, cwg_ref, cbg_ref, g, grp)
        rows = pl.ds(g * gp, grp)
        y = y_scr[rows, :] + u_scr[rows, :] * bias_ref[...]
        o_ref[0, pl.ds(g * grp, grp), :] = (gate * y).astype(o_ref.dtype)


def _conv2_body(u_ref, g_ref, cwu_ref, cbu_ref, cwg_ref, cbg_ref, bias_ref, kf_ref,
                f1_ref, f2_ref, i1_ref, l_ref, o_ref, pad_scr, u_scr, y_scr, a_scr, b_scr,
                *, n, k1n, p, conv_u):
    n2 = HY_N2
    s_rows = n // n2
    rb = min(n, 512)
    gp, ap, bp = _pitch(n2), _pitch(2 * p), _pitch(2 * n2)
    _zero_pad_rows(pad_scr, n)
    _load_u(u_ref, cwu_ref, cbu_ref, pad_scr, u_scr, n, rb, conv_u, n2, gp)

    def stage1(s2, carry):
        rows = u_scr[pl.ds(s2, s_rows, stride=gp), :]
        a = jnp.dot(f1_ref[s2], rows.astype(BF16), preferred_element_type=F32)
        a_scr[pl.ds(pl.multiple_of(s2 * ap, 8), 2 * p), :] = a
        return carry

    lax.fori_loop(0, n2, stage1, 0, unroll=HY_UNROLL)

    for k1 in range(k1n, p):
        b_scr[pl.ds(k1 * bp, 2 * n2), :] = jnp.zeros((2 * n2, b_scr.shape[1]), F32)

    def stage2(k1, carry):
        re = a_scr[pl.ds(k1, n2, stride=ap), :]
        im = a_scr[pl.ds(p + k1, n2, stride=ap), :]
        x = jnp.dot(f2_ref[...], jnp.concatenate([re, im], axis=0).astype(BF16), preferred_element_type=F32)
        kf = kf_ref[k1]
        xr, xi, kr, ki = x[:n2], x[n2:], kf[:n2], kf[n2:]
        prod = jnp.concatenate([xr * kr - xi * ki, xr * ki + xi * kr], axis=0).astype(BF16)
        b_scr[pl.ds(pl.multiple_of(k1 * bp, 8), 2 * n2), :] = jnp.dot(
            i1_ref[...], prod, preferred_element_type=F32)
        return carry

    lax.fori_loop(0, k1n, stage2, 0, unroll=HY_UNROLL)

    def stage3(s2, carry):
        re = b_scr[pl.ds(s2, p, stride=bp), :]
        im = b_scr[pl.ds(n2 + s2, p, stride=bp), :]
        y = jnp.dot(l_ref[s2], jnp.concatenate([re, im], axis=0).astype(BF16), preferred_element_type=F32)
        y_scr[pl.ds(s2, s_rows, stride=gp), :] = y
        return carry

    lax.fori_loop(0, n2, stage3, 0, unroll=HY_UNROLL)
    _gate_out(g_ref, cwg_ref, cbg_ref, bias_ref, pad_scr, u_scr, y_scr, o_ref, n, rb, n2, gp)


def _conv1_body(u_ref, g_ref, cwu_ref, cbu_ref, cwg_ref, cbg_ref, bias_ref, kf_ref,
                f_ref, inv_ref, o_ref, pad_scr, u_scr, y_scr, *, n, p, conv_u):
    _zero_pad_rows(pad_scr, n)
    _load_u(u_ref, cwu_ref, cbu_ref, pad_scr, u_scr, n, n, conv_u, n, n)
    x = jnp.dot(f_ref[...], u_scr[...].astype(BF16), preferred_element_type=F32)
    kf = kf_ref[...]
    xr, xi, kr, ki = x[:p], x[p:], kf[:p], kf[p:]
    prod = jnp.concatenate([xr * kr - xi * ki, xr * ki + xi * kr], axis=0).astype(BF16)
    y_scr[...] = jnp.dot(inv_ref[...], prod, preferred_element_type=F32)
    _gate_out(g_ref, cwg_ref, cbg_ref, bias_ref, pad_scr, u_scr, y_scr, o_ref, n, n, n, n)


def _hy_long_conv(u_arr, u_blk, conv_u, g_arr, g_blk, conv_w, conv_b, bias, kf, order, tabs, two_stage):
    b, n, _ = g_arr.shape
    c = HY_WIDTH
    cb = HY_CB if two_stage else HY_CB1
    per = c // cb
    ublk = u_blk * per
    gblk = g_blk * per
    wu = ((u_blk - BLK_HY) if conv_u else 0) * per
    wg = (g_blk - BLK_HY) * per
    seq = lambda off: pl.BlockSpec((1, n, cb), lambda j, i: (i, 0, off + j))
    wrow = lambda rows, off: pl.BlockSpec((rows, cb), lambda j, i: (0, off + j))
    const = lambda shape: pl.BlockSpec(shape, lambda j, i: (0,) * len(shape))
    common = [seq(ublk), seq(gblk), wrow(HY_SHORT, wu), wrow(1, wu), wrow(HY_SHORT, wg), wrow(1, wg),
              wrow(1, order * per)]
    args = [u_arr, g_arr, conv_w, conv_b, conv_w, conv_b, bias.reshape(1, HY_ORDER * c)]
    seq_rows = (n // HY_N2) * _pitch(HY_N2) if two_stage else n
    scratch = [pltpu.VMEM((n + 2 * HY_PAD, cb), F32), pltpu.VMEM((seq_rows, cb), F32),
               pltpu.VMEM((seq_rows, cb), F32)]
    if two_stage:
        k1n, p = tabs["k1n"], tabs["p"]
        n2 = HY_N2
        body = functools.partial(_conv2_body, n=n, k1n=k1n, p=p, conv_u=conv_u)
        specs = common + [pl.BlockSpec((k1n, 2 * n2, cb), lambda j, i: (0, 0, order * per + j)),
                          const((n2, 2 * p, n // n2)), const((2 * n2, 2 * n2)), const((2 * n2, 2 * n2)),
                          const((n2, n // n2, 2 * p))]
        args += [kf, tabs["first"], tabs["second"], tabs["second_inv"], tabs["last"]]
        scratch += [pltpu.VMEM((n2 * _pitch(2 * p), cb), F32), pltpu.VMEM((p * _pitch(2 * n2), cb), F32)]
        name = "hy_conv2"
    else:
        p = tabs["p"]
        body = functools.partial(_conv1_body, n=n, p=p, conv_u=conv_u)
        specs = common + [pl.BlockSpec((2 * p, cb), lambda j, i: (0, order * per + j)),
                          const((2 * p, n)), const((n, 2 * p))]
        args += [kf, tabs["fwd"], tabs["inv"]]
        name = "hy_conv1"
    return pl.pallas_call(
        body,
        grid=(per, b),
        in_specs=specs,
        out_specs=pl.BlockSpec((1, n, cb), lambda j, i: (i, 0, j)),
        out_shape=jax.ShapeDtypeStruct((b, n, c), BF16),
        scratch_shapes=scratch,
        compiler_params=_cparams("parallel", "parallel"),
        name=name,
    )(*args)


def _hyena_branch(big, conv_w, conv_b, bias, filt_params, two_stage):
    b, n, _ = big.shape
    kfilt = _hy_filter(n, *filt_params)
    if two_stage:
        tabs = _two_stage_tables(n, 2 * n)
        kf = _hy_spectrum2(kfilt, tabs)
    else:
        tabs = _one_stage_tables(n)
        kf = _hy_spectrum1(kfilt, tabs)
    cw = conv_w.astype(F32)
    cbias = conv_b.astype(F32).reshape(1, -1)
    z = _hy_long_conv(big, BLK_HY, True, big, BLK_HY + 1, cw, cbias, bias, kf, 0, tabs, two_stage)
    return _hy_long_conv(z, 0, False, big, BLK_HY + 2, cw, cbias, bias, kf, 1, tabs, two_stage)


def _rope_tables(n):
    rows = n // GRID_W
    row = jnp.repeat(jnp.arange(rows), GRID_W)
    col = jnp.tile(jnp.arange(GRID_W), rows)
    quarter = DA_HEAD_DIM // 4
    inv = ROPE_BASE ** (-jnp.arange(quarter, dtype=F32) / quarter)
    ang = jnp.stack([row, col], axis=-1).astype(F32)[..., None] * inv
    cos, sin = jnp.cos(ang), jnp.sin(ang)
    cos_h = jnp.concatenate([cos, cos], axis=-1).reshape(n, DA_HEAD_DIM)
    sin_h = jnp.concatenate([-sin, sin], axis=-1).reshape(n, DA_HEAD_DIM)
    return jnp.tile(cos_h, (1, 2)), jnp.tile(sin_h, (1, 2))


def _attn_body(*refs, nseg, lam_init):
    lp_ref, g_ref, q_ref = refs[:3]
    k_refs = refs[3:3 + nseg]
    v_refs = refs[3 + nseg:3 + 2 * nseg]
    o_ref = refs[3 + 2 * nseg]
    lp = lp_ref[...]
    lam = (jnp.exp(jnp.sum(lp[0:1] * lp[1:2], axis=1, keepdims=True))
           - jnp.exp(jnp.sum(lp[2:3] * lp[3:4], axis=1, keepdims=True)) + lam_init)
    scale = DA_HEAD_DIM ** -0.5 * math.log2(math.e)
    tq = q_ref.shape[1]
    sub = min(tq, ATT_SUB)
    lane = lax.broadcasted_iota(jnp.int32, (sub, 2 * DA_HEAD_DIM), 1)
    tiles = [pl.ds(r, sub) for r in range(0, tq, sub)]
    scores = []
    for rows in tiles:
        q = (q_ref[0, rows, :].astype(F32) * scale).astype(BF16)
        per_map = []
        for mp in range(2):
            qm = jnp.where((lane // DA_HEAD_DIM) == mp, q, jnp.zeros_like(q))
            per_map.append([lax.dot_general(qm, k_ref[0], (((1,), (1,)), ((), ())),
                                            preferred_element_type=F32) for k_ref in k_refs])
        scores.append(per_map)
    weights = []
    for per_map in scores:
        probs = []
        for s in per_map:
            mx = functools.reduce(jnp.maximum, [jnp.max(t, axis=1, keepdims=True) for t in s])
            e = [jnp.exp2(t - mx) for t in s]
            tot = functools.reduce(jnp.add, [jnp.sum(t, axis=1, keepdims=True) for t in e])
            probs.append((e, 1.0 / tot))
        (e0, r0), (e1, r1) = probs
        r1 = r1 * lam
        weights.append([(t0 * r0 - t1 * r1).astype(BF16) for t0, t1 in zip(e0, e1)])
    for rows, a in zip(tiles, weights):
        o = None
        for t, v_ref in zip(a, v_refs):
            part = jnp.dot(t, v_ref[0], preferred_element_type=F32)
            o = part if o is None else o + part
        o_ref[0, rows, :] = (_rms(o, g_ref[...]) * (1.0 - lam_init)).astype(o_ref.dtype)


def _diff_attention(q_arr, q_blk0, kv_segs, lam_p, norm_g, lam_init):
    b, lq, _ = q_arr.shape
    tq = min(lq, ATT_TQ)
    nseg = len(kv_segs)
    hw = 2 * DA_HEAD_DIM
    specs = [pl.BlockSpec((4, DA_HEAD_DIM), lambda i, h, m: (0, 0)),
             pl.BlockSpec((1, hw), lambda i, h, m: (0, 0)),
             pl.BlockSpec((1, tq, hw), lambda i, h, m: (i, m, q_blk0 + h))]
    args = [lam_p.astype(F32), norm_g.reshape(1, hw), q_arr]
    for k_arr, k_blk0, _, _ in kv_segs:
        specs.append(pl.BlockSpec((1, k_arr.shape[1], hw), functools.partial(
            lambda i, h, m, o: (i, 0, o + h), o=k_blk0)))
        args.append(k_arr)
    for _, _, v_arr, v_blk0 in kv_segs:
        specs.append(pl.BlockSpec((1, v_arr.shape[1], hw), functools.partial(
            lambda i, h, m, o: (i, 0, o + h), o=v_blk0)))
        args.append(v_arr)
    return pl.pallas_call(
        functools.partial(_attn_body, nseg=nseg, lam_init=lam_init),
        grid=(b, DA_HEADS, lq // tq),
        in_specs=specs,
        out_specs=pl.BlockSpec((1, tq, hw), lambda i, h, m: (i, m, h)),
        out_shape=jax.ShapeDtypeStruct((b, lq, DA_WIDTH), BF16),
        compiler_params=_cparams("parallel", "parallel", "arbitrary"),
        name="diff_attention",
    )(*args)


def _merge_body(s_ref, h_ref, a_ref, g0_ref, g1_ref, g2_ref, wb_ref, wo_ref, x_ref, m_ref, ng_ref, o_ref):
    mixed = None
    for br, gl, i in ((s_ref, g0_ref, 0), (h_ref, g1_ref, 1), (a_ref, g2_ref, 2)):
        t = jax.nn.sigmoid(gl[0].astype(F32)) * jnp.dot(br[0], wb_ref[i], preferred_element_type=F32)
        mixed = t if mixed is None else mixed + t
    out = jnp.dot(mixed.astype(BF16), wo_ref[...], preferred_element_type=F32)
    o_ref[0] = x_ref[0] + m_ref[0] * _rms(out, ng_ref[...])


def _merge_residual(ssd, hy, da, big, w_branch, w_out, x, mod_gate, norm_g):
    b, l, d = x.shape
    tm = min(l, 512)
    row = lambda col: pl.BlockSpec((1, tm, d), lambda i, m: (i, m, col))
    return pl.pallas_call(
        _merge_body,
        grid=(b, l // tm),
        in_specs=[row(0), row(0), row(0), row(BLK_GATE), row(BLK_GATE + 1), row(BLK_GATE + 2),
                  pl.BlockSpec((3, d, d), lambda i, m: (0, 0, 0)),
                  pl.BlockSpec((d, d), lambda i, m: (0, 0)),
                  row(0),
                  pl.BlockSpec((1, 1, d), lambda i, m: (i, 0, 0)),
                  pl.BlockSpec((1, d), lambda i, m: (0, 0))],
        out_specs=row(0),
        out_shape=jax.ShapeDtypeStruct((b, l, d), F32),
        compiler_params=_cparams("parallel", "parallel"),
        name="merge_residual",
    )(ssd, hy, da, big, big, big, w_branch, w_out, x, mod_gate.reshape(b, 1, d), norm_g.reshape(1, d))


def _ffn_down_body(gt_ref, p_ref, n_ref, up_ref, cw_ref, cb_ref, wd_ref, x_ref, m_ref, ng_ref, o_ref):
    m, nm = pl.program_id(1), pl.num_programs(1)
    gate = _conv_rows(gt_ref[0].astype(F32), p_ref[0].astype(F32), n_ref[0].astype(F32),
                      cw_ref[...], cb_ref[...], m, nm)
    act = (_silu(gate) * up_ref[0].astype(F32)).astype(BF16)
    out = jnp.dot(act, wd_ref[...], preferred_element_type=F32)
    o_ref[0] = x_ref[0] + m_ref[0] * _rms(out, ng_ref[...])


def _ffn_down_residual(up, conv_w, conv_b, w_down, x, mod_gate, norm_g):
    b, l, d = x.shape
    f = D_FF
    tm = min(l, 512)
    prev, nxt = _halo_specs(tm, l, f, 0)
    return pl.pallas_call(
        _ffn_down_body,
        grid=(b, l // tm),
        in_specs=[pl.BlockSpec((1, tm, f), lambda i, m: (i, m, 0)), prev, nxt,
                  pl.BlockSpec((1, tm, f), lambda i, m: (i, m, 1)),
                  pl.BlockSpec((FFN_CONV, f), lambda i, m: (0, 0)),
                  pl.BlockSpec((1, f), lambda i, m: (0, 0)),
                  pl.BlockSpec((f, d), lambda i, m: (0, 0)),
                  pl.BlockSpec((1, tm, d), lambda i, m: (i, m, 0)),
                  pl.BlockSpec((1, 1, d), lambda i, m: (i, 0, 0)),
                  pl.BlockSpec((1, d), lambda i, m: (0, 0))],
        out_specs=pl.BlockSpec((1, tm, d), lambda i, m: (i, m, 0)),
        out_shape=jax.ShapeDtypeStruct((b, l, d), F32),
        compiler_params=_cparams("parallel", "parallel"),
        name="ffn_down_residual",
    )(up, up, up, up, conv_w.astype(F32), conv_b.astype(F32).reshape(1, f), w_down, x,
      mod_gate.reshape(b, 1, d), norm_g.reshape(1, d))


def _project(x, g, shift, scale, w_big, w_xbc, w_dt, rope=None):
    big = _norm_mod_matmul(x, g, shift, scale, w_big, PROJ_TM, D_MODEL, BF16, "proj_big", rope)
    xbc = _norm_mod_matmul(x, g, shift, scale, w_xbc, PROJ_TM, 512, BF16, "proj_xbc")
    dt = _norm_mod_matmul(x, g, shift, scale, w_dt, PROJ_TM, LANES, F32, "proj_dt")
    return big, xbc, dt


def _lane_row(v):
    flat = v.astype(F32).reshape(1, 2 * SSD_HEADS)
    return jnp.pad(flat, ((0, 0), (0, LANES - 2 * SSD_HEADS)))


def kernel(x, c, ctx, c_ctx, w_ada, b_ada, norm_g, w_in, ssd_conv_w, ssd_conv_b, ssd_a_log, ssd_dt_bias, ssd_d, ssd_norm, hy_conv_w, hy_conv_b, hy_w1, hy_b1, hy_f1, hy_w2, hy_b2, hy_f2, hy_w3, hy_bias, da_lambda, da_norm, w_branch, w_out, ffn_w_up, ffn_conv_w, ffn_conv_b, ffn_w_down):
    b, seq, d = x.shape
    depth = w_ada.shape[0]
    cos_t, sin_t = _rope_tables(seq)
    mod = _ada_mod(c, c_ctx, w_ada, b_ada)
    x_l, x_c = x, ctx
    hshape = (b, SSD_GROUPS, SSD_STATE, SSD_GROUP_W)
    for i in range(depth):
        ctx_out = i < depth - 1
        ml = [mod[i, :b, k * d:(k + 1) * d] for k in range(6)]
        mc = [jnp.broadcast_to(mod[i, b:b + 1, k * d:(k + 1) * d], (b, d)) for k in range(6)]
        ng = norm_g[i]
        wi = w_in[i]
        w_big = jnp.concatenate([wi[:, :SSD_INNER], wi[:, OFF_HY:]], axis=1).astype(BF16)
        w_xbc = wi[:, SSD_INNER:SSD_INNER + SSD_CONV_CH].astype(BF16)
        w_dt = jnp.pad(wi[:, SSD_INNER + SSD_CONV_CH:IN_SSD], ((0, 0), (0, LANES - 2 * SSD_HEADS))).astype(BF16)
        big_c, xbc_c, dt_c = _project(x_c, ng[0], mc[0], mc[1], w_big, w_xbc, w_dt)
        big_l, xbc_l, dt_l = _project(x_l, ng[0], ml[0], ml[1], w_big, w_xbc, w_dt,
                                      (cos_t, sin_t, (BLK_Q, BLK_K)))

        conv_w = ssd_conv_w[i].astype(F32)
        dtb, alog = _lane_row(ssd_dt_bias[i]), _lane_row(ssd_a_log[i])
        dskip_x = jnp.repeat(ssd_d[i].astype(F32), SSD_HEAD_DIM).reshape(1, SSD_INNER)
        h0 = jnp.zeros(hshape, F32)
        ssd_c, hf, hb = _ssd_branch(_ssd_prep(xbc_c, conv_w, ssd_conv_b[i].astype(F32)), dt_c, big_c,
                                    dtb, alog, dskip_x, ssd_norm[i], h0, h0)
        ssd_l, _, _ = _ssd_branch(_ssd_prep(xbc_l, conv_w, ssd_conv_b[i].astype(F32)), dt_l, big_l,
                                  dtb, alog, dskip_x, ssd_norm[i], hf, hb)

        filt = (hy_w1[i], hy_b1[i], hy_f1[i], hy_w2[i], hy_b2[i], hy_f2[i], hy_w3[i])
        hy_l = _hyena_branch(big_l, hy_conv_w[i], hy_conv_b[i], hy_bias[i].astype(F32), filt, True)

        lam_init = 0.8 - 0.6 * math.exp(-0.3 * i)
        hpb = D_MODEL // (2 * DA_HEAD_DIM)
        segs = [(big_c, BLK_K * hpb, big_c, BLK_V * hpb), (big_l, BLK_K * hpb, big_l, BLK_V * hpb)]
        da_l = _diff_attention(big_l, BLK_Q * hpb, segs, da_lambda[i], da_norm[i], lam_init)

        wb = w_branch[i].astype(BF16)
        wo = w_out[i].astype(BF16)
        w_up = ffn_w_up[i].astype(BF16)
        w_dn = ffn_w_down[i].astype(BF16)
        x_l = _merge_residual(ssd_l, hy_l, da_l, big_l, wb, wo, x_l, ml[2], ng[1])
        up_l = _norm_mod_matmul(x_l, ng[2], ml[3], ml[4], w_up, FFN_TM, 1408, BF16, "ffn_up")
        x_l = _ffn_down_residual(up_l, ffn_conv_w[i], ffn_conv_b[i], w_dn, x_l, ml[5], ng[3])
        if ctx_out:
            hy_c = _hyena_branch(big_c, hy_conv_w[i], hy_conv_b[i], hy_bias[i].astype(F32), filt, False)
            da_c = _diff_attention(big_c, BLK_Q * hpb, [(big_c, BLK_K * hpb, big_c, BLK_V * hpb)],
                                   da_lambda[i], da_norm[i], lam_init)
            x_c = _merge_residual(ssd_c, hy_c, da_c, big_c, wb, wo, x_c, mc[2], ng[1])
            up_c = _norm_mod_matmul(x_c, ng[2], mc[3], mc[4], w_up, FFN_TM, 1408, BF16, "ffn_up")
            x_c = _ffn_down_residual(up_c, ffn_conv_w[i], ffn_conv_b[i], w_dn, x_c, mc[5], ng[3])
    return x_l
```

```python
import functools
import math

import numpy as np
import jax
import jax.numpy as jnp
from jax import lax
from jax.experimental import pallas as pl
from jax.experimental.pallas import tpu as pltpu

F32 = jnp.float32
BF16 = jnp.bfloat16
HIGHEST = lax.Precision.HIGHEST

D_MODEL = 1024
GRID_W = 64
EPS = 1e-6

SSD_INNER = D_MODEL
SSD_HEAD_DIM = 64
SSD_HEADS = SSD_INNER // SSD_HEAD_DIM
SSD_GROUPS = 2
SSD_STATE = 128
SSD_CONV = 5
SSD_CHUNK = 128
SSD_CONV_CH = SSD_INNER + 2 * SSD_GROUPS * SSD_STATE
SSD_GROUP_W = SSD_INNER // SSD_GROUPS

HY_WIDTH = D_MODEL
HY_ORDER = 2
HY_SHORT = 3
HY_BANDS = 16
HY_EMB = 1 + 2 * HY_BANDS
HY_FF = 64
HY_TARGET = 1e-2
HY_FAST = 0.3
HY_SLOW = 1.5
HY_N2 = 64
HY_CB = 256
HY_CB_SPEC = 128
HY_CB1 = 512
HY_PAD = 8
HY_UNROLL = 16

DA_HEAD_DIM = 64
DA_HEADS = D_MODEL // (2 * DA_HEAD_DIM)
DA_WIDTH = DA_HEADS * 2 * DA_HEAD_DIM
ROPE_BASE = 10000.0
ATT_TQ = 512
ATT_SUB = 256
D_FF = ((8 * D_MODEL // 3 + 127) // 128) * 128
FFN_CONV = 3

IN_SSD = SSD_INNER + SSD_CONV_CH + 2 * SSD_HEADS
IN_HY = (HY_ORDER + 1) * HY_WIDTH
IN_DA = 3 * DA_WIDTH
OFF_HY = IN_SSD
OFF_DA = OFF_HY + IN_HY
OFF_GATE = OFF_DA + IN_DA

BLK_Z, BLK_HY, BLK_Q, BLK_K, BLK_V, BLK_GATE = 0, 1, 4, 5, 6, 7
BIG_COLS = 10 * D_MODEL

VMEM_LIMIT = 56 * 1024 * 1024
HY_VMEM_LIMIT = 58 * 1024 * 1024
PROJ_TM = 2048
FFN_TM = 1024
LANES = 128
HALO = 16


def _cparams(*sem):
    return pltpu.CompilerParams(dimension_semantics=sem, vmem_limit_bytes=VMEM_LIMIT)


def _silu(x):
    return x * jax.nn.sigmoid(x)


def _rms(x, g):
    return x * lax.rsqrt(jnp.mean(x * x, axis=-1, keepdims=True) + EPS) * g


def _round_up(a, m):
    return (a + m - 1) // m * m


def _pitch(rows):
    p8 = _round_up(rows, 8) // 8
    return 8 * (p8 if p8 % 2 else p8 + 1)


def _ada_body(s_ref, w_ref, b_ref, o_ref):
    s = _silu(s_ref[...])
    o_ref[0] = jnp.dot(s, w_ref[0], preferred_element_type=F32, precision=HIGHEST) + b_ref[0]


def _ada_mod(c, c_ctx, w_ada, b_ada):
    depth, d, n6 = w_ada.shape
    b = c.shape[0]
    rows = _round_up(b + 1, 8)
    s = jnp.zeros((rows, d), F32).at[:b].set(c).at[b].set(c_ctx)
    tn = 1536
    return pl.pallas_call(
        _ada_body,
        grid=(depth, n6 // tn),
        in_specs=[pl.BlockSpec((rows, d), lambda l, j: (0, 0)),
                  pl.BlockSpec((1, d, tn), lambda l, j: (l, 0, j)),
                  pl.BlockSpec((1, 1, tn), lambda l, j: (l, 0, j))],
        out_specs=pl.BlockSpec((1, rows, tn), lambda l, j: (l, 0, j)),
        out_shape=jax.ShapeDtypeStruct((depth, rows, n6), F32),
        compiler_params=_cparams("parallel", "parallel"),
        name="ada_mod",
    )(s, w_ada, b_ada.reshape(depth, 1, n6))


def _nmm_body(x_ref, g_ref, sh_ref, sc_ref, w_ref, o_ref, h_scr):
    @pl.when(pl.program_id(2) == 0)
    def _():
        h = _rms(x_ref[0], g_ref[...]) * (1.0 + sc_ref[0]) + sh_ref[0]
        h_scr[...] = h.astype(BF16)

    o_ref[0] = jnp.dot(h_scr[...], w_ref[...], preferred_element_type=F32).astype(o_ref.dtype)


def _rotate_heads(x, cos, sin):
    quarter = DA_HEAD_DIM // 4
    lane = lax.broadcasted_iota(jnp.int32, cos.shape, 1)
    first_half = (lane % (2 * quarter)) < quarter
    out = []
    for k in range(x.shape[1] // LANES):
        xk = x[:, k * LANES:(k + 1) * LANES]
        xb = xk.astype(BF16)
        partner = jnp.where(first_half, pltpu.roll(xb, LANES - quarter, 1), pltpu.roll(xb, quarter, 1))
        out.append(xk * cos + partner.astype(F32) * sin)
    return jnp.concatenate(out, axis=1)


def _nmm_rope_body(x_ref, g_ref, sh_ref, sc_ref, w_ref, c_ref, s_ref, o_ref, h_scr, *, rope_blocks):
    j = pl.program_id(2)

    @pl.when(j == 0)
    def _():
        h = _rms(x_ref[0], g_ref[...]) * (1.0 + sc_ref[0]) + sh_ref[0]
        h_scr[...] = h.astype(BF16)

    acc = jnp.dot(h_scr[...], w_ref[...], preferred_element_type=F32)
    rotated = functools.reduce(jnp.logical_or, [j == blk for blk in rope_blocks])

    @pl.when(rotated)
    def _():
        o_ref[0] = _rotate_heads(acc, c_ref[...], s_ref[...]).astype(o_ref.dtype)

    @pl.when(jnp.logical_not(rotated))
    def _():
        o_ref[0] = acc.astype(o_ref.dtype)


def _norm_mod_matmul(x, g, shift, scale, w, tm, tn, out_dtype, name, rope=None):
    b, l, d = x.shape
    n = w.shape[1]
    tm = min(l, tm)
    specs = [pl.BlockSpec((1, tm, d), lambda i, m, j: (i, m, 0)),
             pl.BlockSpec((1, d), lambda i, m, j: (0, 0)),
             pl.BlockSpec((1, 1, d), lambda i, m, j: (i, 0, 0)),
             pl.BlockSpec((1, 1, d), lambda i, m, j: (i, 0, 0)),
             pl.BlockSpec((d, tn), lambda i, m, j: (0, j))]
    args = [x, g.reshape(1, d), shift.reshape(b, 1, d), scale.reshape(b, 1, d), w]
    body = _nmm_body
    if rope is not None:
        cos_t, sin_t, blocks = rope
        specs += [pl.BlockSpec((tm, LANES), lambda i, m, j: (m, 0))] * 2
        args += [cos_t, sin_t]
        body = functools.partial(_nmm_rope_body, rope_blocks=blocks)
    return pl.pallas_call(
        body,
        grid=(b, l // tm, n // tn),
        in_specs=specs,
        out_specs=pl.BlockSpec((1, tm, tn), lambda i, m, j: (i, m, j)),
        out_shape=jax.ShapeDtypeStruct((b, l, n), out_dtype),
        scratch_shapes=[pltpu.VMEM((tm, d), BF16)],
        compiler_params=_cparams("parallel", "parallel", "arbitrary"),
        name=name,
    )(*args)


def _halo_specs(tl, l, c, col):
    per = tl // HALO
    last = l // HALO - 1
    prev = pl.BlockSpec((1, HALO, c), lambda i, m: (i, jnp.maximum(m * per - 1, 0), col))
    nxt = pl.BlockSpec((1, HALO, c), lambda i, m: (i, jnp.minimum((m + 1) * per, last), col))
    return prev, nxt


def _conv_rows(x, prev, nxt, w, bias, m, nm):
    tl = x.shape[0]
    k = w.shape[0]
    pad = k // 2
    prev = prev * (m > 0).astype(F32)
    nxt = nxt * (m < nm - 1).astype(F32)
    ext = jnp.concatenate([prev, x, nxt], axis=0)
    rows = tl + 2 * HALO
    y = bias
    for j in range(k):
        shifted = ext if j == pad else pltpu.roll(ext, (pad - j) % rows, 0)
        y = y + shifted[HALO:HALO + tl] * w[j:j + 1]
    return y


def _ssd_prep_body(x_ref, p_ref, n_ref, w_ref, b_ref, o_ref):
    m, nm = pl.program_id(1), pl.num_programs(1)
    y = _conv_rows(x_ref[0].astype(F32), p_ref[0].astype(F32), n_ref[0].astype(F32),
                   w_ref[...], b_ref[...], m, nm)
    o_ref[0] = _silu(y).astype(o_ref.dtype)


def _ssd_prep(xbc_raw, conv_w, conv_b):
    b, l, c = xbc_raw.shape
    tl = min(l, 512)
    prev, nxt = _halo_specs(tl, l, c, 0)
    return pl.pallas_call(
        _ssd_prep_body,
        grid=(b, l // tl),
        in_specs=[pl.BlockSpec((1, tl, c), lambda i, m: (i, m, 0)), prev, nxt,
                  pl.BlockSpec((SSD_CONV, c), lambda i, m: (0, 0)),
                  pl.BlockSpec((1, c), lambda i, m: (0, 0))],
        out_specs=pl.BlockSpec((1, tl, c), lambda i, m: (i, m, 0)),
        out_shape=jax.ShapeDtypeStruct((b, l, c), BF16),
        compiler_params=_cparams("parallel", "parallel"),
        name="ssd_prep",
    )(xbc_raw, xbc_raw, xbc_raw, conv_w, conv_b.reshape(1, c))


def _softplus(x):
    return jnp.maximum(x, 0.0) + jnp.log1p(jnp.exp(-jnp.abs(x)))


def _ssd_chunk(xbc, dt_raw, dtb, alog, h_scr, reverse, col0):
    cs = xbc.shape[0]
    gw = SSD_GROUPS * SSD_STATE
    x = xbc[:, :SSD_INNER].astype(F32)
    dt = _softplus(dt_raw + dtb)
    da = dt * (-jnp.exp(alog))
    row = lax.broadcasted_iota(jnp.int32, (cs, cs), 0)
    col = lax.broadcasted_iota(jnp.int32, (cs, cs), 1)
    keep = (row <= col) if reverse else (row >= col)
    acum = jnp.dot(keep.astype(F32), da, preferred_element_type=F32, precision=HIGHEST)
    acum_t = acum.T
    head_of_lane = col0 + lax.broadcasted_iota(jnp.int32, (2 * LANES, SSD_INNER), 1) // SSD_HEAD_DIM
    src_lane = lax.broadcasted_iota(jnp.int32, (2 * LANES, SSD_INNER), 0) % LANES
    expand = (src_lane == head_of_lane).astype(BF16)

    def per_head_lanes(v):
        hi = v.astype(BF16)
        lo = (v - hi.astype(F32)).astype(BF16)
        return jnp.dot(jnp.concatenate([hi, lo], axis=1), expand, preferred_element_type=F32)

    end = 0 if reverse else cs - 1
    into = per_head_lanes(jnp.exp(acum))
    carry_w = per_head_lanes(jnp.exp(acum[end:end + 1] - acum))
    dec = into[end:end + 1]
    xdt = x * per_head_lanes(dt)
    xw = (xdt * carry_w).astype(BF16)
    xdt_b = xdt.astype(BF16)
    lane = lax.broadcasted_iota(jnp.int32, (cs, LANES), 1)
    parts = []
    for g in range(SSD_GROUPS):
        bm = xbc[:, SSD_INNER + g * SSD_STATE:SSD_INNER + (g + 1) * SSD_STATE]
        cm = xbc[:, SSD_INNER + gw + g * SSD_STATE:SSD_INNER + gw + (g + 1) * SSD_STATE]
        cb = lax.dot_general(cm, bm, (((1,), (1,)), ((), ())), preferred_element_type=F32)
        sl = slice(g * SSD_GROUP_W, (g + 1) * SSD_GROUP_W)
        h_t = h_scr[g]
        y_off = jnp.dot(cm, h_t.astype(BF16), preferred_element_type=F32) * into[:, sl]
        bm_t = bm.astype(F32).T.astype(BF16)
        h_scr[g] = h_t * dec[:, sl] + jnp.dot(bm_t, xw[:, sl], preferred_element_type=F32)
        for p in range(SSD_GROUP_W // LANES):
            lo = g * SSD_GROUP_W + p * LANES
            xp = xdt_b[:, lo:lo + LANES]
            ys = []
            for q in range(LANES // SSD_HEAD_DIM):
                hc = col0 + lo // SSD_HEAD_DIM + q
                seg = acum[:, hc:hc + 1] - acum_t[hc:hc + 1, :]
                dmat = (cb * jnp.exp(jnp.where(keep, seg, -1e30))).astype(BF16)
                ys.append(jnp.dot(dmat, xp, preferred_element_type=F32))
            y_diag = jnp.where(lane < SSD_HEAD_DIM, ys[0], ys[1])
            parts.append(y_diag + y_off[:, p * LANES:(p + 1) * LANES])
    return jnp.concatenate(parts, axis=1), x


def _ssd_fwd_body(xbc_ref, dt_ref, dtb_ref, alog_ref, h0_ref, y_ref, hl_ref, h_scr):
    c = pl.program_id(1)

    @pl.when(c == 0)
    def _():
        h_scr[...] = h0_ref[0]

    y, _ = _ssd_chunk(xbc_ref[0], dt_ref[0], dtb_ref[...], alog_ref[...], h_scr, False, 0)
    y_ref[0] = y

    @pl.when(c == pl.num_programs(1) - 1)
    def _():
        hl_ref[0] = h_scr[...]


def _ssd_bwd_body(xbc_ref, dt_ref, dtb_ref, alog_ref, h0_ref, yf_ref, z_ref, dsk_ref, ng_ref,
                  o_ref, hl_ref, h_scr):
    c = pl.program_id(1)

    @pl.when(c == 0)
    def _():
        h_scr[...] = h0_ref[0]

    yb, x = _ssd_chunk(xbc_ref[0], dt_ref[0], dtb_ref[...], alog_ref[...], h_scr, True, SSD_HEADS)
    y = yf_ref[0] + yb + x * dsk_ref[...]
    o_ref[0] = _rms(y * _silu(z_ref[0].astype(F32)), ng_ref[...]).astype(o_ref.dtype)

    @pl.when(c == pl.num_programs(1) - 1)
    def _():
        hl_ref[0] = h_scr[...]


def _ssd_branch(xbc_act, dt_raw, big, dtb, alog, dskip_x, norm_g, h0f, h0b):
    b, l, _ = xbc_act.shape
    cs = SSD_CHUNK
    nc = l // cs
    hshape = (SSD_GROUPS, SSD_STATE, SSD_GROUP_W)
    state_spec = pl.BlockSpec((1,) + hshape, lambda i, c: (i, 0, 0, 0))
    vec = lambda w: pl.BlockSpec((1, w), lambda i, c: (0, 0))
    state_shape = jax.ShapeDtypeStruct((b,) + hshape, F32)
    yf, hf = pl.pallas_call(
        _ssd_fwd_body,
        grid=(b, nc),
        in_specs=[pl.BlockSpec((1, cs, SSD_CONV_CH), lambda i, c: (i, c, 0)),
                  pl.BlockSpec((1, cs, LANES), lambda i, c: (i, c, 0)),
                  vec(LANES), vec(LANES), state_spec],
        out_specs=[pl.BlockSpec((1, cs, SSD_INNER), lambda i, c: (i, c, 0)), state_spec],
        out_shape=[jax.ShapeDtypeStruct((b, l, SSD_INNER), F32), state_shape],
        scratch_shapes=[pltpu.VMEM(hshape, F32)],
        compiler_params=_cparams("parallel", "arbitrary"),
        name="ssd_scan_fwd",
    )(xbc_act, dt_raw, dtb, alog, h0f)
    rev = lambda i, c: (i, nc - 1 - c, 0)
    out, hb = pl.pallas_call(
        _ssd_bwd_body,
        grid=(b, nc),
        in_specs=[pl.BlockSpec((1, cs, SSD_CONV_CH), rev),
                  pl.BlockSpec((1, cs, LANES), rev),
                  vec(LANES), vec(LANES), state_spec,
                  pl.BlockSpec((1, cs, SSD_INNER), rev),
                  pl.BlockSpec((1, cs, SSD_INNER), lambda i, c: (i, nc - 1 - c, BLK_Z)),
                  vec(SSD_INNER), vec(SSD_INNER)],
        out_specs=[pl.BlockSpec((1, cs, SSD_INNER), rev), state_spec],
        out_shape=[jax.ShapeDtypeStruct((b, l, SSD_INNER), BF16), state_shape],
        scratch_shapes=[pltpu.VMEM(hshape, F32)],
        compiler_params=_cparams("parallel", "arbitrary"),
        name="ssd_scan_bwd",
    )(xbc_act, dt_raw, dtb, alog, h0b, yf, big, dskip_x, norm_g.reshape(1, SSD_INNER))
    return out, hf, hb


def _hy_filter_body(z_ref, w1_ref, b1_ref, f1_ref, w2_ref, b2_ref, f2_ref, w3_ref, dl_ref, o_ref, *, n, tr):
    z = z_ref[...]
    h = jnp.sin(f1_ref[...] * (jnp.dot(z, w1_ref[...], preferred_element_type=F32, precision=HIGHEST)
                               + b1_ref[...]))
    h = jnp.sin(f2_ref[...] * (jnp.dot(h, w2_ref[...], preferred_element_type=F32, precision=HIGHEST)
                               + b2_ref[...]))
    filt = jnp.dot(h.astype(BF16), w3_ref[...], preferred_element_type=F32)
    filt = filt * jnp.exp(-z[:, 0:1] * dl_ref[...])
    rows = pl.program_id(0) * tr + lax.broadcasted_iota(jnp.int32, (tr, 1), 0)
    o_ref[...] = jnp.where(rows == n, 0.0, filt)


def _hy_features(n):
    t = jnp.linspace(0.0, 1.0, n, dtype=F32)[:, None]
    w = (2.0 * math.pi / n) * jnp.arange(n, dtype=F32)[:, None]
    bands = jnp.linspace(1e-4, HY_BANDS - 1, HY_BANDS, dtype=F32)
    z = jnp.concatenate([t, jnp.cos(bands * w), -jnp.sin(bands * w)], axis=-1)
    return jnp.pad(z, ((0, 0), (0, LANES - HY_EMB)))


def _hy_filter(n, w1, b1, f1, w2, b2, f2, w3):
    z = _hy_features(n)
    zk = jnp.concatenate([z, z[:1], jnp.flip(z[1:], axis=0)], axis=0)
    padw = lambda a, r, c: jnp.pad(a.astype(F32), ((0, r - a.shape[0]), (0, c - a.shape[1])))
    w1p = padw(w1, LANES, LANES)
    w2p = padw(w2, LANES, LANES)
    w3p = padw(w3, LANES, w3.shape[1]).astype(BF16)
    rowp = lambda a: padw(a.reshape(1, -1), 1, LANES)
    deltas = jnp.abs(jnp.linspace(math.log(HY_TARGET) / HY_SLOW, math.log(HY_TARGET) / HY_FAST,
                                  HY_WIDTH, dtype=F32))
    dl = jnp.tile(deltas, HY_ORDER).reshape(1, HY_ORDER * HY_WIDTH)
    tr = min(n, 512)
    wide = HY_ORDER * HY_WIDTH
    full = lambda r, c: pl.BlockSpec((r, c), lambda i: (0, 0))
    return pl.pallas_call(
        functools.partial(_hy_filter_body, n=n, tr=tr),
        grid=(2 * n // tr,),
        in_specs=[pl.BlockSpec((tr, LANES), lambda i: (i, 0)),
                  full(LANES, LANES), full(1, LANES), full(1, LANES),
                  full(LANES, LANES), full(1, LANES), full(1, LANES),
                  pl.BlockSpec((LANES, wide), lambda i: (0, (i * tr) // n)),
                  full(1, wide)],
        out_specs=pl.BlockSpec((tr, wide), lambda i: (i, 0)),
        out_shape=jax.ShapeDtypeStruct((2 * n, wide), F32),
        compiler_params=_cparams("parallel"),
        name="hy_filter",
    )(zk, w1p, rowp(b1), rowp(f1), w2p, rowp(b2), rowp(f2), w3p, dl)


def _two_stage_tables(n, filt_rows):
    big_n = 2 * n
    n2 = HY_N2
    n1 = big_n // n2
    k1n = n1 // 2 + 1
    p = _round_up(k1n, 8)
    s2 = np.arange(n2)
    k1 = np.arange(k1n)

    def first(s_rows):
        s1 = np.arange(s_rows)
        ang = 2.0 * np.pi * ((k1[None, :, None] * (n2 * s1[None, None, :] + s2[:, None, None])) % big_n) / big_n
        out = np.zeros((n2, 2 * p, s_rows), np.float32)
        out[:, :k1n] = np.cos(ang)
        out[:, p:p + k1n] = -np.sin(ang)
        return out

    ang2 = 2.0 * np.pi * ((s2[:, None] * s2[None, :]) % n2) / n2
    c2, sn2 = np.cos(ang2), np.sin(ang2)
    second = np.block([[c2, sn2], [-sn2, c2]]).astype(np.float32)
    second_inv = np.block([[c2, -sn2], [sn2, c2]]).astype(np.float32)
    s1o = np.arange(n // n2)
    wgt = np.where((k1 == 0) | (k1 == n1 // 2), 1.0, 2.0) / big_n
    ang = 2.0 * np.pi * ((k1[None, None, :] * (n2 * s1o[None, :, None] + s2[:, None, None])) % big_n) / big_n
    last = np.zeros((n2, n // n2, 2 * p), np.float32)
    last[:, :, :k1n] = wgt * np.cos(ang)
    last[:, :, p:p + k1n] = -wgt * np.sin(ang)
    cast = lambda a: jnp.asarray(a).astype(BF16)
    return dict(k1n=k1n, p=p, first=cast(first(n // n2)), first_filt=cast(first(filt_rows // n2)),
                second=cast(second), second_inv=cast(second_inv), last=cast(last))


def _one_stage_tables(n):
    big_n = 2 * n
    kn = n + 1
    p = _round_up(kn, 8)
    k = np.arange(kn)

    def fwd(rows):
        s = np.arange(rows)
        ang = 2.0 * np.pi * ((k[:, None] * s[None, :]) % big_n) / big_n
        out = np.zeros((2 * p, rows), np.float32)
        out[:kn] = np.cos(ang)
        out[p:p + kn] = -np.sin(ang)
        return out

    s = np.arange(n)
    wgt = np.where((k == 0) | (k == n), 1.0, 2.0) / big_n
    ang = 2.0 * np.pi * ((s[:, None] * k[None, :]) % big_n) / big_n
    inv = np.zeros((n, 2 * p), np.float32)
    inv[:, :kn] = wgt * np.cos(ang)
    inv[:, p:p + kn] = -wgt * np.sin(ang)
    cast = lambda a: jnp.asarray(a).astype(BF16)
    return dict(p=p, fwd=cast(fwd(n)), fwd_filt=cast(fwd(big_n)), inv=cast(inv))


def _spec2_body(k_ref, f1_ref, f2_ref, o_ref, k_scr, a_scr, *, k1n, p, s_rows):
    n2 = HY_N2
    ap, gp = _pitch(2 * p), _pitch(n2)
    for g in range(s_rows):
        k_scr[pl.ds(g * gp, n2), :] = k_ref[pl.ds(g * n2, n2), :]

    def stage1(s2, carry):
        rows = k_scr[pl.ds(s2, s_rows, stride=gp), :]
        a = jnp.dot(f1_ref[s2], rows.astype(BF16), preferred_element_type=F32)
        a_scr[pl.ds(pl.multiple_of(s2 * ap, 8), 2 * p), :] = a
        return carry

    lax.fori_loop(0, n2, stage1, 0, unroll=HY_UNROLL)

    def stage2(k1, carry):
        re = a_scr[pl.ds(k1, n2, stride=ap), :]
        im = a_scr[pl.ds(p + k1, n2, stride=ap), :]
        a = jnp.concatenate([re, im], axis=0).astype(BF16)
        o_ref[k1] = jnp.dot(f2_ref[...], a, preferred_element_type=F32)
        return carry

    lax.fori_loop(0, k1n, stage2, 0, unroll=HY_UNROLL)


def _hy_spectrum2(kfilt, tabs):
    rows, wide = kfilt.shape
    k1n, p = tabs["k1n"], tabs["p"]
    s_rows = rows // HY_N2
    cb = HY_CB_SPEC
    return pl.pallas_call(
        functools.partial(_spec2_body, k1n=k1n, p=p, s_rows=s_rows),
        grid=(wide // cb,),
        in_specs=[pl.BlockSpec((rows, cb), lambda j: (0, j)),
                  pl.BlockSpec((HY_N2, 2 * p, s_rows), lambda j: (0, 0, 0)),
                  pl.BlockSpec((2 * HY_N2, 2 * HY_N2), lambda j: (0, 0))],
        out_specs=pl.BlockSpec((k1n, 2 * HY_N2, cb), lambda j: (0, 0, j)),
        out_shape=jax.ShapeDtypeStruct((k1n, 2 * HY_N2, wide), F32),
        scratch_shapes=[pltpu.VMEM((s_rows * _pitch(HY_N2), cb), F32),
                        pltpu.VMEM((HY_N2 * _pitch(2 * p), cb), F32)],
        compiler_params=_cparams("parallel"),
        name="hy_spectrum2",
    )(kfilt, tabs["first_filt"], tabs["second"])


def _spec1_body(k_ref, f_ref, o_ref):
    o_ref[...] = jnp.dot(f_ref[...], k_ref[...].astype(BF16), preferred_element_type=F32)


def _hy_spectrum1(kfilt, tabs):
    rows, wide = kfilt.shape
    p = tabs["p"]
    cb = HY_CB1
    return pl.pallas_call(
        _spec1_body,
        grid=(wide // cb,),
        in_specs=[pl.BlockSpec((rows, cb), lambda j: (0, j)),
                  pl.BlockSpec((2 * p, rows), lambda j: (0, 0))],
        out_specs=pl.BlockSpec((2 * p, cb), lambda j: (0, j)),
        out_shape=jax.ShapeDtypeStruct((2 * p, wide), F32),
        compiler_params=_cparams("parallel"),
        name="hy_spectrum1",
    )(kfilt, tabs["fwd_filt"])


def _slab(sl):
    return slice(sl * LANES, (sl + 1) * LANES)


def _fill_pad(src_ref, pad_scr, n, rb):
    for sl in range(pad_scr.shape[0]):
        for r in range(0, n, rb):
            pad_scr[sl, pl.ds(HY_PAD + r, rb), :] = src_ref[0, pl.ds(r, rb), _slab(sl)].astype(F32)


def _zero_pad_rows(pad_scr, n):
    zeros = jnp.zeros((HY_PAD, LANES), F32)
    for sl in range(pad_scr.shape[0]):
        pad_scr[sl, pl.ds(0, HY_PAD), :] = zeros
        pad_scr[sl, pl.ds(HY_PAD + n, HY_PAD), :] = zeros


def _conv3_group(pad_scr, w_ref, b_ref, g, grp, sl):
    acc = b_ref[:, _slab(sl)]
    for j in range(HY_SHORT):
        acc = acc + pad_scr[sl, pl.ds(HY_PAD - 1 + j + g * grp, grp), :] * w_ref[j:j + 1, _slab(sl)]
    return acc


def _rows(ref, start, size, stride=None):
    idx = pl.ds(start, size) if stride is None else pl.ds(start, size, stride=stride)
    parts = [ref[sl, idx, :] for sl in range(ref.shape[0])]
    return parts[0] if len(parts) == 1 else jnp.concatenate(parts, axis=1)


def _set_rows(ref, start, size, val, stride=None):
    idx = pl.ds(start, size) if stride is None else pl.ds(start, size, stride=stride)
    for sl in range(ref.shape[0]):
        ref[sl, idx, :] = val[:, sl * LANES:(sl + 1) * LANES]


def _load_u(u_ref, cwu_ref, cbu_ref, pad_scr, u_scr, n, rb, conv_u, grp, gp):
    _fill_pad(u_ref, pad_scr, n, rb)
    for sl in range(pad_scr.shape[0]):
        for g in range(n // grp):
            if conv_u:
                val = _conv3_group(pad_scr, cwu_ref, cbu_ref, g, grp, sl)
            else:
                val = pad_scr[sl, pl.ds(HY_PAD + g * grp, grp), :]
            u_scr[sl, pl.ds(g * gp, grp), :] = val


def _gate_out(g_ref, cwg_ref, cbg_ref, bias_ref, pad_scr, u_scr, y_scr, o_ref, n, rb, grp, gp):
    _fill_pad(g_ref, pad_scr, n, rb)
    for sl in range(pad_scr.shape[0]):
        for g in range(n // grp):
            gate = _conv3_group(pad_scr, cwg_ref, cbg_ref, g, grp, sl)
            rows = pl.ds(g * gp, grp)
            y = y_scr[sl, rows, :] + u_scr[sl, rows, :] * bias_ref[:, _slab(sl)]
            o_ref[0, pl.ds(g * grp, grp), _slab(sl)] = (gate * y).astype(o_ref.dtype)


def _conv2_body(u_ref, g_ref, cwu_ref, cbu_ref, cwg_ref, cbg_ref, bias_ref, kf_ref,
                f1_ref, f2_ref, i1_ref, l_ref, o_ref, pad_scr, u_scr, a_scr, b_scr,
                *, n, k1n, p, conv_u):
    n2 = HY_N2
    y_scr = a_scr
    s_rows = n // n2
    rb = min(n, 512)
    gp, ap, bp = _pitch(n2), _pitch(2 * p), _pitch(2 * n2)
    _zero_pad_rows(pad_scr, n)
    _load_u(u_ref, cwu_ref, cbu_ref, pad_scr, u_scr, n, rb, conv_u, n2, gp)

    def stage1(s2, carry):
        rows = _rows(u_scr, s2, s_rows, gp)
        a = jnp.dot(f1_ref[s2], rows.astype(BF16), preferred_element_type=F32)
        _set_rows(a_scr, pl.multiple_of(s2 * ap, 8), 2 * p, a)
        return carry

    lax.fori_loop(0, n2, stage1, 0, unroll=HY_UNROLL)

    for k1 in range(k1n, p):
        _set_rows(b_scr, k1 * bp, 2 * n2, jnp.zeros((2 * n2, b_scr.shape[0] * LANES), F32))

    def stage2(k1, carry):
        re = _rows(a_scr, k1, n2, ap)
        im = _rows(a_scr, p + k1, n2, ap)
        x = jnp.dot(f2_ref[...], jnp.concatenate([re, im], axis=0).astype(BF16), preferred_element_type=F32)
        kf = kf_ref[k1]
        xr, xi, kr, ki = x[:n2], x[n2:], kf[:n2], kf[n2:]
        prod = jnp.concatenate([xr * kr - xi * ki, xr * ki + xi * kr], axis=0).astype(BF16)
        _set_rows(b_scr, pl.multiple_of(k1 * bp, 8), 2 * n2,
                  jnp.dot(i1_ref[...], prod, preferred_element_type=F32))
        return carry

    lax.fori_loop(0, k1n, stage2, 0, unroll=HY_UNROLL)

    def stage3(s2, carry):
        re = _rows(b_scr, s2, p, bp)
        im = _rows(b_scr, n2 + s2, p, bp)
        y = jnp.dot(l_ref[s2], jnp.concatenate([re, im], axis=0).astype(BF16), preferred_element_type=F32)
        _set_rows(y_scr, s2, s_rows, y, gp)
        return carry

    lax.fori_loop(0, n2, stage3, 0, unroll=HY_UNROLL)
    _gate_out(g_ref, cwg_ref, cbg_ref, bias_ref, pad_scr, u_scr, y_scr, o_ref, n, rb, n2, gp)


def _conv1_body(u_ref, g_ref, cwu_ref, cbu_ref, cwg_ref, cbg_ref, bias_ref, kf_ref,
                f_ref, inv_ref, o_ref, pad_scr, u_scr, y_scr, *, n, p, conv_u):
    _zero_pad_rows(pad_scr, n)
    _load_u(u_ref, cwu_ref, cbu_ref, pad_scr, u_scr, n, n, conv_u, n, n)
    x = jnp.dot(f_ref[...], _rows(u_scr, 0, n).astype(BF16), preferred_element_type=F32)
    kf = kf_ref[...]
    xr, xi, kr, ki = x[:p], x[p:], kf[:p], kf[p:]
    prod = jnp.concatenate([xr * kr - xi * ki, xr * ki + xi * kr], axis=0).astype(BF16)
    _set_rows(y_scr, 0, n, jnp.dot(inv_ref[...], prod, preferred_element_type=F32))
    _gate_out(g_ref, cwg_ref, cbg_ref, bias_ref, pad_scr, u_scr, y_scr, o_ref, n, n, n, n)


def _hy_long_conv(u_arr, u_blk, conv_u, g_arr, g_blk, conv_w, conv_b, bias, kf, order, tabs, two_stage):
    b, n, _ = g_arr.shape
    c = HY_WIDTH
    cb = HY_CB if two_stage else HY_CB1
    per = c // cb
    ublk = u_blk * per
    gblk = g_blk * per
    wu = ((u_blk - BLK_HY) if conv_u else 0) * per
    wg = (g_blk - BLK_HY) * per
    seq = lambda off: pl.BlockSpec((1, n, cb), lambda j, i: (i, 0, off + j))
    wrow = lambda rows, off: pl.BlockSpec((rows, cb), lambda j, i: (0, off + j))
    once = pl.Buffered(1)
    const = lambda shape: pl.BlockSpec(shape, lambda j, i: (0,) * len(shape), pipeline_mode=once)
    common = [seq(ublk), seq(gblk), wrow(HY_SHORT, wu), wrow(1, wu), wrow(HY_SHORT, wg), wrow(1, wg),
              wrow(1, order * per)]
    args = [u_arr, g_arr, conv_w, conv_b, conv_w, conv_b, bias.reshape(1, HY_ORDER * c)]
    seq_rows = (n // HY_N2) * _pitch(HY_N2) if two_stage else n
    slabs = cb // LANES
    scratch = [pltpu.VMEM((slabs, n + 2 * HY_PAD, LANES), F32), pltpu.VMEM((slabs, seq_rows, LANES), F32)]
    if not two_stage:
        scratch.append(pltpu.VMEM((slabs, seq_rows, LANES), F32))
    if two_stage:
        k1n, p = tabs["k1n"], tabs["p"]
        n2 = HY_N2
        body = functools.partial(_conv2_body, n=n, k1n=k1n, p=p, conv_u=conv_u)
        specs = common + [pl.BlockSpec((k1n, 2 * n2, cb), lambda j, i: (0, 0, order * per + j),
                                       pipeline_mode=once),
                          const((n2, 2 * p, n // n2)), const((2 * n2, 2 * n2)), const((2 * n2, 2 * n2)),
                          const((n2, n // n2, 2 * p))]
        args += [kf, tabs["first"], tabs["second"], tabs["second_inv"], tabs["last"]]
        scratch += [pltpu.VMEM((slabs, n2 * _pitch(2 * p), LANES), F32),
                    pltpu.VMEM((slabs, p * _pitch(2 * n2), LANES), F32)]
        name = "hy_conv2"
    else:
        p = tabs["p"]
        body = functools.partial(_conv1_body, n=n, p=p, conv_u=conv_u)
        specs = common + [pl.BlockSpec((2 * p, cb), lambda j, i: (0, order * per + j)),
                          const((2 * p, n)), const((n, 2 * p))]
        args += [kf, tabs["fwd"], tabs["inv"]]
        name = "hy_conv1"
    return pl.pallas_call(
        body,
        grid=(per, b),
        in_specs=specs,
        out_specs=pl.BlockSpec((1, n, cb), lambda j, i: (i, 0, j)),
        out_shape=jax.ShapeDtypeStruct((b, n, c), BF16),
        scratch_shapes=scratch,
        compiler_params=pltpu.CompilerParams(dimension_semantics=("parallel", "parallel"),
                                             vmem_limit_bytes=HY_VMEM_LIMIT),
        name=name,
    )(*args)


def _hyena_branch(big, conv_w, conv_b, bias, filt_params, two_stage):
    b, n, _ = big.shape
    kfilt = _hy_filter(n, *filt_params)
    if two_stage:
        tabs = _two_stage_tables(n, 2 * n)
        kf = _hy_spectrum2(kfilt, tabs)
    else:
        tabs = _one_stage_tables(n)
        kf = _hy_spectrum1(kfilt, tabs)
    cw = conv_w.astype(F32)
    cbias = conv_b.astype(F32).reshape(1, -1)
    z = _hy_long_conv(big, BLK_HY, True, big, BLK_HY + 1, cw, cbias, bias, kf, 0, tabs, two_stage)
    return _hy_long_conv(z, 0, False, big, BLK_HY + 2, cw, cbias, bias, kf, 1, tabs, two_stage)


def _rope_tables(n):
    rows = n // GRID_W
    row = jnp.repeat(jnp.arange(rows), GRID_W)
    col = jnp.tile(jnp.arange(GRID_W), rows)
    quarter = DA_HEAD_DIM // 4
    inv = ROPE_BASE ** (-jnp.arange(quarter, dtype=F32) / quarter)
    ang = jnp.stack([row, col], axis=-1).astype(F32)[..., None] * inv
    cos, sin = jnp.cos(ang), jnp.sin(ang)
    cos_h = jnp.concatenate([cos, cos], axis=-1).reshape(n, DA_HEAD_DIM)
    sin_h = jnp.concatenate([-sin, sin], axis=-1).reshape(n, DA_HEAD_DIM)
    return jnp.tile(cos_h, (1, 2)), jnp.tile(sin_h, (1, 2))


def _attn_body(*refs, nseg, lam_init):
    lp_ref, g_ref, q_ref = refs[:3]
    k_refs = refs[3:3 + nseg]
    v_refs = refs[3 + nseg:3 + 2 * nseg]
    o_ref = refs[3 + 2 * nseg]
    lp = lp_ref[...]
    lam = (jnp.exp(jnp.sum(lp[0:1] * lp[1:2], axis=1, keepdims=True))
           - jnp.exp(jnp.sum(lp[2:3] * lp[3:4], axis=1, keepdims=True)) + lam_init)
    scale = DA_HEAD_DIM ** -0.5 * math.log2(math.e)
    tq = q_ref.shape[1]
    sub = min(tq, ATT_SUB)
    lane = lax.broadcasted_iota(jnp.int32, (sub, 2 * DA_HEAD_DIM), 1)
    tiles = [pl.ds(r, sub) for r in range(0, tq, sub)]
    scores = []
    for rows in tiles:
        q = (q_ref[0, rows, :].astype(F32) * scale).astype(BF16)
        per_map = []
        for mp in range(2):
            qm = jnp.where((lane // DA_HEAD_DIM) == mp, q, jnp.zeros_like(q))
            per_map.append([lax.dot_general(qm, k_ref[0], (((1,), (1,)), ((), ())),
                                            preferred_element_type=F32) for k_ref in k_refs])
        scores.append(per_map)
    weights = []
    for per_map in scores:
        probs = []
        for s in per_map:
            mx = functools.reduce(jnp.maximum, [jnp.max(t, axis=1, keepdims=True) for t in s])
            e = [jnp.exp2(t - mx) for t in s]
            tot = functools.reduce(jnp.add, [jnp.sum(t, axis=1, keepdims=True) for t in e])
            probs.append((e, 1.0 / tot))
        (e0, r0), (e1, r1) = probs
        r1 = r1 * lam
        weights.append([(t0 * r0 - t1 * r1).astype(BF16) for t0, t1 in zip(e0, e1)])
    for rows, a in zip(tiles, weights):
        o = None
        for t, v_ref in zip(a, v_refs):
            part = jnp.dot(t, v_ref[0], preferred_element_type=F32)
            o = part if o is None else o + part
        o_ref[0, rows, :] = (_rms(o, g_ref[...]) * (1.0 - lam_init)).astype(o_ref.dtype)


def _diff_attention(q_arr, q_blk0, kv_segs, lam_p, norm_g, lam_init):
    b, lq, _ = q_arr.shape
    tq = min(lq, ATT_TQ)
    nseg = len(kv_segs)
    hw = 2 * DA_HEAD_DIM
    specs = [pl.BlockSpec((4, DA_HEAD_DIM), lambda i, h, m: (0, 0)),
             pl.BlockSpec((1, hw), lambda i, h, m: (0, 0)),
             pl.BlockSpec((1, tq, hw), lambda i, h, m: (i, m, q_blk0 + h))]
    args = [lam_p.astype(F32), norm_g.reshape(1, hw), q_arr]
    for k_arr, k_blk0, _, _ in kv_segs:
        specs.append(pl.BlockSpec((1, k_arr.shape[1], hw), functools.partial(
            lambda i, h, m, o: (i, 0, o + h), o=k_blk0)))
        args.append(k_arr)
    for _, _, v_arr, v_blk0 in kv_segs:
        specs.append(pl.BlockSpec((1, v_arr.shape[1], hw), functools.partial(
            lambda i, h, m, o: (i, 0, o + h), o=v_blk0)))
        args.append(v_arr)
    return pl.pallas_call(
        functools.partial(_attn_body, nseg=nseg, lam_init=lam_init),
        grid=(b, DA_HEADS, lq // tq),
        in_specs=specs,
        out_specs=pl.BlockSpec((1, tq, hw), lambda i, h, m: (i, m, h)),
        out_shape=jax.ShapeDtypeStruct((b, lq, DA_WIDTH), BF16),
        compiler_params=_cparams("parallel", "parallel", "arbitrary"),
        name="diff_attention",
    )(*args)


def _merge_body(s_ref, h_ref, a_ref, g0_ref, g1_ref, g2_ref, wb_ref, wo_ref, x_ref, m_ref, ng_ref, o_ref):
    mixed = None
    for br, gl, i in ((s_ref, g0_ref, 0), (h_ref, g1_ref, 1), (a_ref, g2_ref, 2)):
        t = jax.nn.sigmoid(gl[0].astype(F32)) * jnp.dot(br[0], wb_ref[i], preferred_element_type=F32)
        mixed = t if mixed is None else mixed + t
    out = jnp.dot(mixed.astype(BF16), wo_ref[...], preferred_element_type=F32)
    o_ref[0] = x_ref[0] + m_ref[0] * _rms(out, ng_ref[...])


def _merge_residual(ssd, hy, da, big, w_branch, w_out, x, mod_gate, norm_g):
    b, l, d = x.shape
    tm = min(l, 512)
    row = lambda col: pl.BlockSpec((1, tm, d), lambda i, m: (i, m, col))
    return pl.pallas_call(
        _merge_body,
        grid=(b, l // tm),
        in_specs=[row(0), row(0), row(0), row(BLK_GATE), row(BLK_GATE + 1), row(BLK_GATE + 2),
                  pl.BlockSpec((3, d, d), lambda i, m: (0, 0, 0)),
                  pl.BlockSpec((d, d), lambda i, m: (0, 0)),
                  row(0),
                  pl.BlockSpec((1, 1, d), lambda i, m: (i, 0, 0)),
                  pl.BlockSpec((1, d), lambda i, m: (0, 0))],
        out_specs=row(0),
        out_shape=jax.ShapeDtypeStruct((b, l, d), F32),
        compiler_params=_cparams("parallel", "parallel"),
        name="merge_residual",
    )(ssd, hy, da, big, big, big, w_branch, w_out, x, mod_gate.reshape(b, 1, d), norm_g.reshape(1, d))


def _ffn_down_body(gt_ref, p_ref, n_ref, up_ref, cw_ref, cb_ref, wd_ref, x_ref, m_ref, ng_ref, o_ref):
    m, nm = pl.program_id(1), pl.num_programs(1)
    gate = _conv_rows(gt_ref[0].astype(F32), p_ref[0].astype(F32), n_ref[0].astype(F32),
                      cw_ref[...], cb_ref[...], m, nm)
    act = (_silu(gate) * up_ref[0].astype(F32)).astype(BF16)
    out = jnp.dot(act, wd_ref[...], preferred_element_type=F32)
    o_ref[0] = x_ref[0] + m_ref[0] * _rms(out, ng_ref[...])


def _ffn_down_residual(up, conv_w, conv_b, w_down, x, mod_gate, norm_g):
    b, l, d = x.shape
    f = D_FF
    tm = min(l, 512)
    prev, nxt = _halo_specs(tm, l, f, 0)
    return pl.pallas_call(
        _ffn_down_body,
        grid=(b, l // tm),
        in_specs=[pl.BlockSpec((1, tm, f), lambda i, m: (i, m, 0)), prev, nxt,
                  pl.BlockSpec((1, tm, f), lambda i, m: (i, m, 1)),
                  pl.BlockSpec((FFN_CONV, f), lambda i, m: (0, 0)),
                  pl.BlockSpec((1, f), lambda i, m: (0, 0)),
                  pl.BlockSpec((f, d), lambda i, m: (0, 0)),
                  pl.BlockSpec((1, tm, d), lambda i, m: (i, m, 0)),
                  pl.BlockSpec((1, 1, d), lambda i, m: (i, 0, 0)),
                  pl.BlockSpec((1, d), lambda i, m: (0, 0))],
        out_specs=pl.BlockSpec((1, tm, d), lambda i, m: (i, m, 0)),
        out_shape=jax.ShapeDtypeStruct((b, l, d), F32),
        compiler_params=_cparams("parallel", "parallel"),
        name="ffn_down_residual",
    )(up, up, up, up, conv_w.astype(F32), conv_b.astype(F32).reshape(1, f), w_down, x,
      mod_gate.reshape(b, 1, d), norm_g.reshape(1, d))


def _project(x, g, shift, scale, w_big, w_xbc, w_dt, rope=None):
    big = _norm_mod_matmul(x, g, shift, scale, w_big, PROJ_TM, D_MODEL, BF16, "proj_big", rope)
    xbc = _norm_mod_matmul(x, g, shift, scale, w_xbc, PROJ_TM, 512, BF16, "proj_xbc")
    dt = _norm_mod_matmul(x, g, shift, scale, w_dt, PROJ_TM, LANES, F32, "proj_dt")
    return big, xbc, dt


def _lane_row(v):
    flat = v.astype(F32).reshape(1, 2 * SSD_HEADS)
    return jnp.pad(flat, ((0, 0), (0, LANES - 2 * SSD_HEADS)))


def kernel(x, c, ctx, c_ctx, w_ada, b_ada, norm_g, w_in, ssd_conv_w, ssd_conv_b, ssd_a_log, ssd_dt_bias, ssd_d, ssd_norm, hy_conv_w, hy_conv_b, hy_w1, hy_b1, hy_f1, hy_w2, hy_b2, hy_f2, hy_w3, hy_bias, da_lambda, da_norm, w_branch, w_out, ffn_w_up, ffn_conv_w, ffn_conv_b, ffn_w_down):
    b, seq, d = x.shape
    depth = w_ada.shape[0]
    cos_t, sin_t = _rope_tables(seq)
    mod = _ada_mod(c, c_ctx, w_ada, b_ada)
    x_l, x_c = x, ctx
    hshape = (b, SSD_GROUPS, SSD_STATE, SSD_GROUP_W)
    for i in range(depth):
        ctx_out = i < depth - 1
        ml = [mod[i, :b, k * d:(k + 1) * d] for k in range(6)]
        mc = [jnp.broadcast_to(mod[i, b:b + 1, k * d:(k + 1) * d], (b, d)) for k in range(6)]
        ng = norm_g[i]
        wi = w_in[i]
        w_big = jnp.concatenate([wi[:, :SSD_INNER], wi[:, OFF_HY:]], axis=1).astype(BF16)
        w_xbc = wi[:, SSD_INNER:SSD_INNER + SSD_CONV_CH].astype(BF16)
        w_dt = jnp.pad(wi[:, SSD_INNER + SSD_CONV_CH:IN_SSD], ((0, 0), (0, LANES - 2 * SSD_HEADS))).astype(BF16)
        big_c, xbc_c, dt_c = _project(x_c, ng[0], mc[0], mc[1], w_big, w_xbc, w_dt)
        big_l, xbc_l, dt_l = _project(x_l, ng[0], ml[0], ml[1], w_big, w_xbc, w_dt,
                                      (cos_t, sin_t, (BLK_Q, BLK_K)))

        conv_w = ssd_conv_w[i].astype(F32)
        dtb, alog = _lane_row(ssd_dt_bias[i]), _lane_row(ssd_a_log[i])
        dskip_x = jnp.repeat(ssd_d[i].astype(F32), SSD_HEAD_DIM).reshape(1, SSD_INNER)
        h0 = jnp.zeros(hshape, F32)
        ssd_c, hf, hb = _ssd_branch(_ssd_prep(xbc_c, conv_w, ssd_conv_b[i].astype(F32)), dt_c, big_c,
                                    dtb, alog, dskip_x, ssd_norm[i], h0, h0)
        ssd_l, _, _ = _ssd_branch(_ssd_prep(xbc_l, conv_w, ssd_conv_b[i].astype(F32)), dt_l, big_l,
                                  dtb, alog, dskip_x, ssd_norm[i], hf, hb)

        filt = (hy_w1[i], hy_b1[i], hy_f1[i], hy_w2[i], hy_b2[i], hy_f2[i], hy_w3[i])
        hy_l = _hyena_branch(big_l, hy_conv_w[i], hy_conv_b[i], hy_bias[i].astype(F32), filt, True)

        lam_init = 0.8 - 0.6 * math.exp(-0.3 * i)
        hpb = D_MODEL // (2 * DA_HEAD_DIM)
        segs = [(big_c, BLK_K * hpb, big_c, BLK_V * hpb), (big_l, BLK_K * hpb, big_l, BLK_V * hpb)]
        da_l = _diff_attention(big_l, BLK_Q * hpb, segs, da_lambda[i], da_norm[i], lam_init)

        wb = w_branch[i].astype(BF16)
        wo = w_out[i].astype(BF16)
        w_up = ffn_w_up[i].astype(BF16)
        w_dn = ffn_w_down[i].astype(BF16)
        x_l = _merge_residual(ssd_l, hy_l, da_l, big_l, wb, wo, x_l, ml[2], ng[1])
        up_l = _norm_mod_matmul(x_l, ng[2], ml[3], ml[4], w_up, FFN_TM, 1408, BF16, "ffn_up")
        x_l = _ffn_down_residual(up_l, ffn_conv_w[i], ffn_conv_b[i], w_dn, x_l, ml[5], ng[3])
        if ctx_out:
            hy_c = _hyena_branch(big_c, hy_conv_w[i], hy_conv_b[i], hy_bias[i].astype(F32), filt, False)
            da_c = _diff_attention(big_c, BLK_Q * hpb, [(big_c, BLK_K * hpb, big_c, BLK_V * hpb)],
                                   da_lambda[i], da_norm[i], lam_init)
            x_c = _merge_residual(ssd_c, hy_c, da_c, big_c, wb, wo, x_c, mc[2], ng[1])
            up_c = _norm_mod_matmul(x_c, ng[2], mc[3], mc[4], w_up, FFN_TM, 1408, BF16, "ffn_up")
            x_c = _ffn_down_residual(up_c, ffn_conv_w[i], ffn_conv_b[i], w_dn, x_c, mc[5], ng[3])
    return x_l
```

```python
import functools
import math

import numpy as np
import jax
import jax.numpy as jnp
from jax import lax
from jax.experimental import pallas as pl
from jax.experimental.pallas import tpu as pltpu

F32 = jnp.float32
BF16 = jnp.bfloat16
HIGHEST = lax.Precision.HIGHEST

D_MODEL = 1024
GRID_W = 64
EPS = 1e-6

SSD_INNER = D_MODEL
SSD_HEAD_DIM = 64
SSD_HEADS = SSD_INNER // SSD_HEAD_DIM
SSD_GROUPS = 2
SSD_STATE = 128
SSD_CONV = 5
SSD_CHUNK = 128
SSD_CONV_CH = SSD_INNER + 2 * SSD_GROUPS * SSD_STATE
SSD_GROUP_W = SSD_INNER // SSD_GROUPS

HY_WIDTH = D_MODEL
HY_ORDER = 2
HY_SHORT = 3
HY_BANDS = 16
HY_EMB = 1 + 2 * HY_BANDS
HY_FF = 64
HY_TARGET = 1e-2
HY_FAST = 0.3
HY_SLOW = 1.5
HY_N2 = 64
HY_CB = 256
HY_CB_SPEC = 128
HY_CB1 = 512
HY_PAD = 8
HY_UNROLL = 16

DA_HEAD_DIM = 64
DA_HEADS = D_MODEL // (2 * DA_HEAD_DIM)
DA_WIDTH = DA_HEADS * 2 * DA_HEAD_DIM
ROPE_BASE = 10000.0
ATT_TQ = 1024
ATT_SUB = 256
D_FF = ((8 * D_MODEL // 3 + 127) // 128) * 128
FFN_CONV = 3

IN_SSD = SSD_INNER + SSD_CONV_CH + 2 * SSD_HEADS
IN_HY = (HY_ORDER + 1) * HY_WIDTH
IN_DA = 3 * DA_WIDTH
OFF_HY = IN_SSD
OFF_DA = OFF_HY + IN_HY
OFF_GATE = OFF_DA + IN_DA

BLK_Z, BLK_HY, BLK_Q, BLK_K, BLK_V, BLK_GATE = 0, 1, 4, 5, 6, 7
BIG_COLS = 10 * D_MODEL

VMEM_LIMIT = 56 * 1024 * 1024
HY_VMEM_LIMIT = 58 * 1024 * 1024
PROJ_TM = 2048
FFN_TM = 1024
LANES = 128
HALO = 16


def _cparams(*sem):
    return pltpu.CompilerParams(dimension_semantics=sem, vmem_limit_bytes=VMEM_LIMIT)


def _silu(x):
    return x * jax.nn.sigmoid(x)


def _rms(x, g):
    return x * lax.rsqrt(jnp.mean(x * x, axis=-1, keepdims=True) + EPS) * g


def _round_up(a, m):
    return (a + m - 1) // m * m


def _pitch(rows):
    p8 = _round_up(rows, 8) // 8
    return 8 * (p8 if p8 % 2 else p8 + 1)


def _ada_body(s_ref, w_ref, b_ref, o_ref):
    s = _silu(s_ref[...])
    o_ref[0] = jnp.dot(s, w_ref[0], preferred_element_type=F32, precision=HIGHEST) + b_ref[0]


def _ada_mod(c, c_ctx, w_ada, b_ada):
    depth, d, n6 = w_ada.shape
    b = c.shape[0]
    rows = _round_up(b + 1, 8)
    s = jnp.zeros((rows, d), F32).at[:b].set(c).at[b].set(c_ctx)
    tn = 1536
    return pl.pallas_call(
        _ada_body,
        grid=(depth, n6 // tn),
        in_specs=[pl.BlockSpec((rows, d), lambda l, j: (0, 0)),
                  pl.BlockSpec((1, d, tn), lambda l, j: (l, 0, j)),
                  pl.BlockSpec((1, 1, tn), lambda l, j: (l, 0, j))],
        out_specs=pl.BlockSpec((1, rows, tn), lambda l, j: (l, 0, j)),
        out_shape=jax.ShapeDtypeStruct((depth, rows, n6), F32),
        compiler_params=_cparams("parallel", "parallel"),
        name="ada_mod",
    )(s, w_ada, b_ada.reshape(depth, 1, n6))


def _nmm_body(x_ref, g_ref, sh_ref, sc_ref, w_ref, o_ref, h_scr):
    @pl.when(pl.program_id(2) == 0)
    def _():
        h = _rms(x_ref[0], g_ref[...]) * (1.0 + sc_ref[0]) + sh_ref[0]
        h_scr[...] = h.astype(BF16)

    o_ref[0] = jnp.dot(h_scr[...], w_ref[...], preferred_element_type=F32).astype(o_ref.dtype)


def _rotate_heads(x, cos, sin):
    quarter = DA_HEAD_DIM // 4
    lane = lax.broadcasted_iota(jnp.int32, cos.shape, 1)
    first_half = (lane % (2 * quarter)) < quarter
    out = []
    for k in range(x.shape[1] // LANES):
        xk = x[:, k * LANES:(k + 1) * LANES]
        xb = xk.astype(BF16)
        partner = jnp.where(first_half, pltpu.roll(xb, LANES - quarter, 1), pltpu.roll(xb, quarter, 1))
        out.append(xk * cos + partner.astype(F32) * sin)
    return jnp.concatenate(out, axis=1)


def _nmm_rope_body(x_ref, g_ref, sh_ref, sc_ref, w_ref, c_ref, s_ref, o_ref, h_scr, *, rope_blocks):
    j = pl.program_id(2)

    @pl.when(j == 0)
    def _():
        h = _rms(x_ref[0], g_ref[...]) * (1.0 + sc_ref[0]) + sh_ref[0]
        h_scr[...] = h.astype(BF16)

    acc = jnp.dot(h_scr[...], w_ref[...], preferred_element_type=F32)
    rotated = functools.reduce(jnp.logical_or, [j == blk for blk in rope_blocks])

    @pl.when(rotated)
    def _():
        o_ref[0] = _rotate_heads(acc, c_ref[...], s_ref[...]).astype(o_ref.dtype)

    @pl.when(jnp.logical_not(rotated))
    def _():
        o_ref[0] = acc.astype(o_ref.dtype)


def _norm_mod_matmul(x, g, shift, scale, w, tm, tn, out_dtype, name, rope=None):
    b, l, d = x.shape
    n = w.shape[1]
    tm = min(l, tm)
    specs = [pl.BlockSpec((1, tm, d), lambda i, m, j: (i, m, 0)),
             pl.BlockSpec((1, d), lambda i, m, j: (0, 0)),
             pl.BlockSpec((1, 1, d), lambda i, m, j: (i, 0, 0)),
             pl.BlockSpec((1, 1, d), lambda i, m, j: (i, 0, 0)),
             pl.BlockSpec((d, tn), lambda i, m, j: (0, j))]
    args = [x, g.reshape(1, d), shift.reshape(b, 1, d), scale.reshape(b, 1, d), w]
    body = _nmm_body
    if rope is not None:
        cos_t, sin_t, blocks = rope
        specs += [pl.BlockSpec((tm, LANES), lambda i, m, j: (m, 0))] * 2
        args += [cos_t, sin_t]
        body = functools.partial(_nmm_rope_body, rope_blocks=blocks)
    return pl.pallas_call(
        body,
        grid=(b, l // tm, n // tn),
        in_specs=specs,
        out_specs=pl.BlockSpec((1, tm, tn), lambda i, m, j: (i, m, j)),
        out_shape=jax.ShapeDtypeStruct((b, l, n), out_dtype),
        scratch_shapes=[pltpu.VMEM((tm, d), BF16)],
        compiler_params=_cparams("parallel", "parallel", "arbitrary"),
        name=name,
    )(*args)


def _halo_specs(tl, l, c, col):
    per = tl // HALO
    last = l // HALO - 1
    prev = pl.BlockSpec((1, HALO, c), lambda i, m: (i, jnp.maximum(m * per - 1, 0), col))
    nxt = pl.BlockSpec((1, HALO, c), lambda i, m: (i, jnp.minimum((m + 1) * per, last), col))
    return prev, nxt


def _conv_rows(x, prev, nxt, w, bias, m, nm):
    tl = x.shape[0]
    k = w.shape[0]
    pad = k // 2
    prev = prev * (m > 0).astype(F32)
    nxt = nxt * (m < nm - 1).astype(F32)
    ext = jnp.concatenate([prev, x, nxt], axis=0)
    rows = tl + 2 * HALO
    y = bias
    for j in range(k):
        shifted = ext if j == pad else pltpu.roll(ext, (pad - j) % rows, 0)
        y = y + shifted[HALO:HALO + tl] * w[j:j + 1]
    return y


def _ssd_prep_body(x_ref, p_ref, n_ref, w_ref, b_ref, o_ref):
    m, nm = pl.program_id(1), pl.num_programs(1)
    y = _conv_rows(x_ref[0].astype(F32), p_ref[0].astype(F32), n_ref[0].astype(F32),
                   w_ref[...], b_ref[...], m, nm)
    o_ref[0] = _silu(y).astype(o_ref.dtype)


def _ssd_prep(xbc_raw, conv_w, conv_b):
    b, l, c = xbc_raw.shape
    tl = min(l, 512)
    prev, nxt = _halo_specs(tl, l, c, 0)
    return pl.pallas_call(
        _ssd_prep_body,
        grid=(b, l // tl),
        in_specs=[pl.BlockSpec((1, tl, c), lambda i, m: (i, m, 0)), prev, nxt,
                  pl.BlockSpec((SSD_CONV, c), lambda i, m: (0, 0)),
                  pl.BlockSpec((1, c), lambda i, m: (0, 0))],
        out_specs=pl.BlockSpec((1, tl, c), lambda i, m: (i, m, 0)),
        out_shape=jax.ShapeDtypeStruct((b, l, c), BF16),
        compiler_params=_cparams("parallel", "parallel"),
        name="ssd_prep",
    )(xbc_raw, xbc_raw, xbc_raw, conv_w, conv_b.reshape(1, c))


def _softplus(x):
    return jnp.maximum(x, 0.0) + jnp.log1p(jnp.exp(-jnp.abs(x)))


def _ssd_chunk(xbc, dt_raw, dtb, alog, h_scr, reverse, col0):
    cs = xbc.shape[0]
    gw = SSD_GROUPS * SSD_STATE
    x = xbc[:, :SSD_INNER].astype(F32)
    dt = _softplus(dt_raw + dtb)
    da = dt * (-jnp.exp(alog))
    row = lax.broadcasted_iota(jnp.int32, (cs, cs), 0)
    col = lax.broadcasted_iota(jnp.int32, (cs, cs), 1)
    keep = (row <= col) if reverse else (row >= col)
    acum = jnp.dot(keep.astype(F32), da, preferred_element_type=F32, precision=HIGHEST)
    acum_t = acum.T
    head_of_lane = col0 + lax.broadcasted_iota(jnp.int32, (2 * LANES, SSD_INNER), 1) // SSD_HEAD_DIM
    src_lane = lax.broadcasted_iota(jnp.int32, (2 * LANES, SSD_INNER), 0) % LANES
    expand = (src_lane == head_of_lane).astype(BF16)

    def per_head_lanes(v):
        hi = v.astype(BF16)
        lo = (v - hi.astype(F32)).astype(BF16)
        return jnp.dot(jnp.concatenate([hi, lo], axis=1), expand, preferred_element_type=F32)

    end = 0 if reverse else cs - 1
    into = per_head_lanes(jnp.exp(acum))
    carry_w = per_head_lanes(jnp.exp(acum[end:end + 1] - acum))
    dec = into[end:end + 1]
    xdt = x * per_head_lanes(dt)
    xw = (xdt * carry_w).astype(BF16)
    xdt_b = xdt.astype(BF16)
    lane = lax.broadcasted_iota(jnp.int32, (cs, LANES), 1)
    parts = []
    for g in range(SSD_GROUPS):
        bm = xbc[:, SSD_INNER + g * SSD_STATE:SSD_INNER + (g + 1) * SSD_STATE]
        cm = xbc[:, SSD_INNER + gw + g * SSD_STATE:SSD_INNER + gw + (g + 1) * SSD_STATE]
        cb = lax.dot_general(cm, bm, (((1,), (1,)), ((), ())), preferred_element_type=F32)
        sl = slice(g * SSD_GROUP_W, (g + 1) * SSD_GROUP_W)
        h_t = h_scr[g]
        y_off = jnp.dot(cm, h_t.astype(BF16), preferred_element_type=F32) * into[:, sl]
        bm_t = bm.astype(F32).T.astype(BF16)
        h_scr[g] = h_t * dec[:, sl] + jnp.dot(bm_t, xw[:, sl], preferred_element_type=F32)
        for p in range(SSD_GROUP_W // LANES):
            lo = g * SSD_GROUP_W + p * LANES
            xp = xdt_b[:, lo:lo + LANES]
            ys = []
            for q in range(LANES // SSD_HEAD_DIM):
                hc = col0 + lo // SSD_HEAD_DIM + q
                seg = acum[:, hc:hc + 1] - acum_t[hc:hc + 1, :]
                dmat = (cb * jnp.exp(jnp.where(keep, seg, -1e30))).astype(BF16)
                ys.append(jnp.dot(dmat, xp, preferred_element_type=F32))
            y_diag = jnp.where(lane < SSD_HEAD_DIM, ys[0], ys[1])
            parts.append(y_diag + y_off[:, p * LANES:(p + 1) * LANES])
    return jnp.concatenate(parts, axis=1), x


def _ssd_fwd_body(xbc_ref, dt_ref, dtb_ref, alog_ref, h0_ref, y_ref, hl_ref, h_scr):
    c = pl.program_id(1)

    @pl.when(c == 0)
    def _():
        h_scr[...] = h0_ref[0]

    y, _ = _ssd_chunk(xbc_ref[0], dt_ref[0], dtb_ref[...], alog_ref[...], h_scr, False, 0)
    y_ref[0] = y

    @pl.when(c == pl.num_programs(1) - 1)
    def _():
        hl_ref[0] = h_scr[...]


def _ssd_bwd_body(xbc_ref, dt_ref, dtb_ref, alog_ref, h0_ref, yf_ref, z_ref, dsk_ref, ng_ref,
                  o_ref, hl_ref, h_scr):
    c = pl.program_id(1)

    @pl.when(c == 0)
    def _():
        h_scr[...] = h0_ref[0]

    yb, x = _ssd_chunk(xbc_ref[0], dt_ref[0], dtb_ref[...], alog_ref[...], h_scr, True, SSD_HEADS)
    y = yf_ref[0] + yb + x * dsk_ref[...]
    o_ref[0] = _rms(y * _silu(z_ref[0].astype(F32)), ng_ref[...]).astype(o_ref.dtype)

    @pl.when(c == pl.num_programs(1) - 1)
    def _():
        hl_ref[0] = h_scr[...]


def _ssd_branch(xbc_act, dt_raw, big, dtb, alog, dskip_x, norm_g, h0f, h0b):
    b, l, _ = xbc_act.shape
    cs = SSD_CHUNK
    nc = l // cs
    hshape = (SSD_GROUPS, SSD_STATE, SSD_GROUP_W)
    state_spec = pl.BlockSpec((1,) + hshape, lambda i, c: (i, 0, 0, 0))
    vec = lambda w: pl.BlockSpec((1, w), lambda i, c: (0, 0))
    state_shape = jax.ShapeDtypeStruct((b,) + hshape, F32)
    yf, hf = pl.pallas_call(
        _ssd_fwd_body,
        grid=(b, nc),
        in_specs=[pl.BlockSpec((1, cs, SSD_CONV_CH), lambda i, c: (i, c, 0)),
                  pl.BlockSpec((1, cs, LANES), lambda i, c: (i, c, 0)),
                  vec(LANES), vec(LANES), state_spec],
        out_specs=[pl.BlockSpec((1, cs, SSD_INNER), lambda i, c: (i, c, 0)), state_spec],
        out_shape=[jax.ShapeDtypeStruct((b, l, SSD_INNER), F32), state_shape],
        scratch_shapes=[pltpu.VMEM(hshape, F32)],
        compiler_params=_cparams("parallel", "arbitrary"),
        name="ssd_scan_fwd",
    )(xbc_act, dt_raw, dtb, alog, h0f)
    rev = lambda i, c: (i, nc - 1 - c, 0)
    out, hb = pl.pallas_call(
        _ssd_bwd_body,
        grid=(b, nc),
        in_specs=[pl.BlockSpec((1, cs, SSD_CONV_CH), rev),
                  pl.BlockSpec((1, cs, LANES), rev),
                  vec(LANES), vec(LANES), state_spec,
                  pl.BlockSpec((1, cs, SSD_INNER), rev),
                  pl.BlockSpec((1, cs, SSD_INNER), lambda i, c: (i, nc - 1 - c, BLK_Z)),
                  vec(SSD_INNER), vec(SSD_INNER)],
        out_specs=[pl.BlockSpec((1, cs, SSD_INNER), rev), state_spec],
        out_shape=[jax.ShapeDtypeStruct((b, l, SSD_INNER), BF16), state_shape],
        scratch_shapes=[pltpu.VMEM(hshape, F32)],
        compiler_params=_cparams("parallel", "arbitrary"),
        name="ssd_scan_bwd",
    )(xbc_act, dt_raw, dtb, alog, h0b, yf, big, dskip_x, norm_g.reshape(1, SSD_INNER))
    return out, hf, hb


def _hy_filter_body(z_ref, w1_ref, b1_ref, f1_ref, w2_ref, b2_ref, f2_ref, w3_ref, dl_ref, o_ref, *, n, tr):
    z = z_ref[...]
    h = jnp.sin(f1_ref[...] * (jnp.dot(z, w1_ref[...], preferred_element_type=F32, precision=HIGHEST)
                               + b1_ref[...]))
    h = jnp.sin(f2_ref[...] * (jnp.dot(h, w2_ref[...], preferred_element_type=F32, precision=HIGHEST)
                               + b2_ref[...]))
    filt = jnp.dot(h.astype(BF16), w3_ref[...], preferred_element_type=F32)
    filt = filt * jnp.exp(-z[:, 0:1] * dl_ref[...])
    rows = pl.program_id(0) * tr + lax.broadcasted_iota(jnp.int32, (tr, 1), 0)
    o_ref[...] = jnp.where(rows == n, 0.0, filt)


def _hy_features(n):
    t = jnp.linspace(0.0, 1.0, n, dtype=F32)[:, None]
    w = (2.0 * math.pi / n) * jnp.arange(n, dtype=F32)[:, None]
    bands = jnp.linspace(1e-4, HY_BANDS - 1, HY_BANDS, dtype=F32)
    z = jnp.concatenate([t, jnp.cos(bands * w), -jnp.sin(bands * w)], axis=-1)
    return jnp.pad(z, ((0, 0), (0, LANES - HY_EMB)))


def _hy_filter(n, w1, b1, f1, w2, b2, f2, w3):
    z = _hy_features(n)
    zk = jnp.concatenate([z, z[:1], jnp.flip(z[1:], axis=0)], axis=0)
    padw = lambda a, r, c: jnp.pad(a.astype(F32), ((0, r - a.shape[0]), (0, c - a.shape[1])))
    w1p = padw(w1, LANES, LANES)
    w2p = padw(w2, LANES, LANES)
    w3p = padw(w3, LANES, w3.shape[1]).astype(BF16)
    rowp = lambda a: padw(a.reshape(1, -1), 1, LANES)
    deltas = jnp.abs(jnp.linspace(math.log(HY_TARGET) / HY_SLOW, math.log(HY_TARGET) / HY_FAST,
                                  HY_WIDTH, dtype=F32))
    dl = jnp.tile(deltas, HY_ORDER).reshape(1, HY_ORDER * HY_WIDTH)
    tr = min(n, 512)
    wide = HY_ORDER * HY_WIDTH
    full = lambda r, c: pl.BlockSpec((r, c), lambda i: (0, 0))
    return pl.pallas_call(
        functools.partial(_hy_filter_body, n=n, tr=tr),
        grid=(2 * n // tr,),
        in_specs=[pl.BlockSpec((tr, LANES), lambda i: (i, 0)),
                  full(LANES, LANES), full(1, LANES), full(1, LANES),
                  full(LANES, LANES), full(1, LANES), full(1, LANES),
                  pl.BlockSpec((LANES, wide), lambda i: (0, (i * tr) // n)),
                  full(1, wide)],
        out_specs=pl.BlockSpec((tr, wide), lambda i: (i, 0)),
        out_shape=jax.ShapeDtypeStruct((2 * n, wide), F32),
        compiler_params=_cparams("parallel"),
        name="hy_filter",
    )(zk, w1p, rowp(b1), rowp(f1), w2p, rowp(b2), rowp(f2), w3p, dl)


def _two_stage_tables(n, filt_rows):
    big_n = 2 * n
    n2 = HY_N2
    n1 = big_n // n2
    k1n = n1 // 2 + 1
    p = _round_up(k1n, 8)
    s2 = np.arange(n2)
    k1 = np.arange(k1n)

    def first(s_rows):
        s1 = np.arange(s_rows)
        ang = 2.0 * np.pi * ((k1[None, :, None] * (n2 * s1[None, None, :] + s2[:, None, None])) % big_n) / big_n
        out = np.zeros((n2, 2 * p, s_rows), np.float32)
        out[:, :k1n] = np.cos(ang)
        out[:, p:p + k1n] = -np.sin(ang)
        return out

    ang2 = 2.0 * np.pi * ((s2[:, None] * s2[None, :]) % n2) / n2
    c2, sn2 = np.cos(ang2), np.sin(ang2)
    second = np.block([[c2, sn2], [-sn2, c2]]).astype(np.float32)
    second_inv = np.block([[c2, -sn2], [sn2, c2]]).astype(np.float32)
    s1o = np.arange(n // n2)
    wgt = np.where((k1 == 0) | (k1 == n1 // 2), 1.0, 2.0) / big_n
    ang = 2.0 * np.pi * ((k1[None, None, :] * (n2 * s1o[None, :, None] + s2[:, None, None])) % big_n) / big_n
    last = np.zeros((n2, n // n2, 2 * p), np.float32)
    last[:, :, :k1n] = wgt * np.cos(ang)
    last[:, :, p:p + k1n] = -wgt * np.sin(ang)
    cast = lambda a: jnp.asarray(a).astype(BF16)
    return dict(k1n=k1n, p=p, first=cast(first(n // n2)), first_filt=cast(first(filt_rows // n2)),
                second=cast(second), second_inv=cast(second_inv), last=cast(last))


def _one_stage_tables(n):
    big_n = 2 * n
    kn = n + 1
    p = _round_up(kn, 8)
    k = np.arange(kn)

    def fwd(rows):
        s = np.arange(rows)
        ang = 2.0 * np.pi * ((k[:, None] * s[None, :]) % big_n) / big_n
        out = np.zeros((2 * p, rows), np.float32)
        out[:kn] = np.cos(ang)
        out[p:p + kn] = -np.sin(ang)
        return out

    s = np.arange(n)
    wgt = np.where((k == 0) | (k == n), 1.0, 2.0) / big_n
    ang = 2.0 * np.pi * ((s[:, None] * k[None, :]) % big_n) / big_n
    inv = np.zeros((n, 2 * p), np.float32)
    inv[:, :kn] = wgt * np.cos(ang)
    inv[:, p:p + kn] = -wgt * np.sin(ang)
    cast = lambda a: jnp.asarray(a).astype(BF16)
    return dict(p=p, fwd=cast(fwd(n)), fwd_filt=cast(fwd(big_n)), inv=cast(inv))


def _spec2_body(k_ref, f1_ref, f2_ref, o_ref, k_scr, a_scr, *, k1n, p, s_rows):
    n2 = HY_N2
    ap, gp = _pitch(2 * p), _pitch(n2)
    for g in range(s_rows):
        k_scr[pl.ds(g * gp, n2), :] = k_ref[pl.ds(g * n2, n2), :]

    def stage1(s2, carry):
        rows = k_scr[pl.ds(s2, s_rows, stride=gp), :]
        a = jnp.dot(f1_ref[s2], rows.astype(BF16), preferred_element_type=F32)
        a_scr[pl.ds(pl.multiple_of(s2 * ap, 8), 2 * p), :] = a
        return carry

    lax.fori_loop(0, n2, stage1, 0, unroll=HY_UNROLL)

    def stage2(k1, carry):
        re = a_scr[pl.ds(k1, n2, stride=ap), :]
        im = a_scr[pl.ds(p + k1, n2, stride=ap), :]
        a = jnp.concatenate([re, im], axis=0).astype(BF16)
        o_ref[k1] = jnp.dot(f2_ref[...], a, preferred_element_type=F32)
        return carry

    lax.fori_loop(0, k1n, stage2, 0, unroll=HY_UNROLL)


def _hy_spectrum2(kfilt, tabs):
    rows, wide = kfilt.shape
    k1n, p = tabs["k1n"], tabs["p"]
    s_rows = rows // HY_N2
    cb = HY_CB_SPEC
    return pl.pallas_call(
        functools.partial(_spec2_body, k1n=k1n, p=p, s_rows=s_rows),
        grid=(wide // cb,),
        in_specs=[pl.BlockSpec((rows, cb), lambda j: (0, j)),
                  pl.BlockSpec((HY_N2, 2 * p, s_rows), lambda j: (0, 0, 0)),
                  pl.BlockSpec((2 * HY_N2, 2 * HY_N2), lambda j: (0, 0))],
        out_specs=pl.BlockSpec((k1n, 2 * HY_N2, cb), lambda j: (0, 0, j)),
        out_shape=jax.ShapeDtypeStruct((k1n, 2 * HY_N2, wide), F32),
        scratch_shapes=[pltpu.VMEM((s_rows * _pitch(HY_N2), cb), F32),
                        pltpu.VMEM((HY_N2 * _pitch(2 * p), cb), F32)],
        compiler_params=_cparams("parallel"),
        name="hy_spectrum2",
    )(kfilt, tabs["first_filt"], tabs["second"])


def _spec1_body(k_ref, f_ref, o_ref):
    o_ref[...] = jnp.dot(f_ref[...], k_ref[...].astype(BF16), preferred_element_type=F32)


def _hy_spectrum1(kfilt, tabs):
    rows, wide = kfilt.shape
    p = tabs["p"]
    cb = HY_CB1
    return pl.pallas_call(
        _spec1_body,
        grid=(wide // cb,),
        in_specs=[pl.BlockSpec((rows, cb), lambda j: (0, j)),
                  pl.BlockSpec((2 * p, rows), lambda j: (0, 0))],
        out_specs=pl.BlockSpec((2 * p, cb), lambda j: (0, j)),
        out_shape=jax.ShapeDtypeStruct((2 * p, wide), F32),
        compiler_params=_cparams("parallel"),
        name="hy_spectrum1",
    )(kfilt, tabs["fwd_filt"])


def _slab(sl):
    return slice(sl * LANES, (sl + 1) * LANES)


def _fill_pad(src_ref, pad_scr, n, rb):
    for sl in range(pad_scr.shape[0]):
        for r in range(0, n, rb):
            pad_scr[sl, pl.ds(HY_PAD + r, rb), :] = src_ref[0, pl.ds(r, rb), _slab(sl)].astype(F32)


def _zero_pad_rows(pad_scr, n):
    zeros = jnp.zeros((HY_PAD, LANES), F32)
    for sl in range(pad_scr.shape[0]):
        pad_scr[sl, pl.ds(0, HY_PAD), :] = zeros
        pad_scr[sl, pl.ds(HY_PAD + n, HY_PAD), :] = zeros


def _conv3_group(pad_scr, w_ref, b_ref, g, grp, sl):
    acc = b_ref[:, _slab(sl)]
    for j in range(HY_SHORT):
        acc = acc + pad_scr[sl, pl.ds(HY_PAD - 1 + j + g * grp, grp), :] * w_ref[j:j + 1, _slab(sl)]
    return acc


def _rows(ref, start, size, stride=None):
    idx = pl.ds(start, size) if stride is None else pl.ds(start, size, stride=stride)
    parts = [ref[sl, idx, :] for sl in range(ref.shape[0])]
    return parts[0] if len(parts) == 1 else jnp.concatenate(parts, axis=1)


def _set_rows(ref, start, size, val, stride=None):
    idx = pl.ds(start, size) if stride is None else pl.ds(start, size, stride=stride)
    for sl in range(ref.shape[0]):
        ref[sl, idx, :] = val[:, sl * LANES:(sl + 1) * LANES]


def _load_u(u_ref, cwu_ref, cbu_ref, pad_scr, u_scr, n, rb, conv_u, grp, gp):
    _fill_pad(u_ref, pad_scr, n, rb)
    for sl in range(pad_scr.shape[0]):
        for g in range(n // grp):
            if conv_u:
                val = _conv3_group(pad_scr, cwu_ref, cbu_ref, g, grp, sl)
            else:
                val = pad_scr[sl, pl.ds(HY_PAD + g * grp, grp), :]
            u_scr[sl, pl.ds(g * gp, grp), :] = val


def _gate_out(g_ref, cwg_ref, cbg_ref, bias_ref, pad_scr, u_scr, y_scr, o_ref, n, rb, grp, gp):
    _fill_pad(g_ref, pad_scr, n, rb)
    for sl in range(pad_scr.shape[0]):
        for g in range(n // grp):
            gate = _conv3_group(pad_scr, cwg_ref, cbg_ref, g, grp, sl)
            rows = pl.ds(g * gp, grp)
            y = y_scr[sl, rows, :] + u_scr[sl, rows, :] * bias_ref[:, _slab(sl)]
            o_ref[0, pl.ds(g * grp, grp), _slab(sl)] = (gate * y).astype(o_ref.dtype)


def _conv2_body(u_ref, g_ref, cwu_ref, cbu_ref, cwg_ref, cbg_ref, bias_ref, kf_ref,
                f1_ref, f2_ref, i1_ref, l_ref, o_ref, pad_scr, u_scr, a_scr, b_scr,
                *, n, k1n, p, conv_u):
    n2 = HY_N2
    y_scr = a_scr
    s_rows = n // n2
    rb = min(n, 512)
    gp, ap, bp = _pitch(n2), _pitch(2 * p), _pitch(2 * n2)
    _zero_pad_rows(pad_scr, n)
    _load_u(u_ref, cwu_ref, cbu_ref, pad_scr, u_scr, n, rb, conv_u, n2, gp)

    def stage1(s2, carry):
        rows = _rows(u_scr, s2, s_rows, gp)
        a = jnp.dot(f1_ref[s2], rows.astype(BF16), preferred_element_type=F32)
        _set_rows(a_scr, pl.multiple_of(s2 * ap, 8), 2 * p, a)
        return carry

    lax.fori_loop(0, n2, stage1, 0, unroll=HY_UNROLL)

    for k1 in range(k1n, p):
        _set_rows(b_scr, k1 * bp, 2 * n2, jnp.zeros((2 * n2, b_scr.shape[0] * LANES), F32))

    def stage2(k1, carry):
        re = _rows(a_scr, k1, n2, ap)
        im = _rows(a_scr, p + k1, n2, ap)
        x = jnp.dot(f2_ref[...], jnp.concatenate([re, im], axis=0).astype(BF16), preferred_element_type=F32)
        kf = kf_ref[k1]
        xr, xi, kr, ki = x[:n2], x[n2:], kf[:n2], kf[n2:]
        prod = jnp.concatenate([xr * kr - xi * ki, xr * ki + xi * kr], axis=0).astype(BF16)
        _set_rows(b_scr, pl.multiple_of(k1 * bp, 8), 2 * n2,
                  jnp.dot(i1_ref[...], prod, preferred_element_type=F32))
        return carry

    lax.fori_loop(0, k1n, stage2, 0, unroll=HY_UNROLL)

    def stage3(s2, carry):
        re = _rows(b_scr, s2, p, bp)
        im = _rows(b_scr, n2 + s2, p, bp)
        y = jnp.dot(l_ref[s2], jnp.concatenate([re, im], axis=0).astype(BF16), preferred_element_type=F32)
        _set_rows(y_scr, s2, s_rows, y, gp)
        return carry

    lax.fori_loop(0, n2, stage3, 0, unroll=HY_UNROLL)
    _gate_out(g_ref, cwg_ref, cbg_ref, bias_ref, pad_scr, u_scr, y_scr, o_ref, n, rb, n2, gp)


def _conv1_body(u_ref, g_ref, cwu_ref, cbu_ref, cwg_ref, cbg_ref, bias_ref, kf_ref,
                f_ref, inv_ref, o_ref, pad_scr, u_scr, y_scr, *, n, p, conv_u):
    _zero_pad_rows(pad_scr, n)
    _load_u(u_ref, cwu_ref, cbu_ref, pad_scr, u_scr, n, n, conv_u, n, n)
    x = jnp.dot(f_ref[...], _rows(u_scr, 0, n).astype(BF16), preferred_element_type=F32)
    kf = kf_ref[...]
    xr, xi, kr, ki = x[:p], x[p:], kf[:p], kf[p:]
    prod = jnp.concatenate([xr * kr - xi * ki, xr * ki + xi * kr], axis=0).astype(BF16)
    _set_rows(y_scr, 0, n, jnp.dot(inv_ref[...], prod, preferred_element_type=F32))
    _gate_out(g_ref, cwg_ref, cbg_ref, bias_ref, pad_scr, u_scr, y_scr, o_ref, n, n, n, n)


def _hy_long_conv(u_arr, u_blk, conv_u, g_arr, g_blk, conv_w, conv_b, bias, kf, order, tabs, two_stage):
    b, n, _ = g_arr.shape
    c = HY_WIDTH
    cb = HY_CB if two_stage else HY_CB1
    per = c // cb
    ublk = u_blk * per
    gblk = g_blk * per
    wu = ((u_blk - BLK_HY) if conv_u else 0) * per
    wg = (g_blk - BLK_HY) * per
    seq = lambda off: pl.BlockSpec((1, n, cb), lambda j, i: (i, 0, off + j))
    wrow = lambda rows, off: pl.BlockSpec((rows, cb), lambda j, i: (0, off + j))
    once = pl.Buffered(1)
    const = lambda shape: pl.BlockSpec(shape, lambda j, i: (0,) * len(shape), pipeline_mode=once)
    common = [seq(ublk), seq(gblk), wrow(HY_SHORT, wu), wrow(1, wu), wrow(HY_SHORT, wg), wrow(1, wg),
              wrow(1, order * per)]
    args = [u_arr, g_arr, conv_w, conv_b, conv_w, conv_b, bias.reshape(1, HY_ORDER * c)]
    seq_rows = (n // HY_N2) * _pitch(HY_N2) if two_stage else n
    slabs = cb // LANES
    scratch = [pltpu.VMEM((slabs, n + 2 * HY_PAD, LANES), F32), pltpu.VMEM((slabs, seq_rows, LANES), F32)]
    if not two_stage:
        scratch.append(pltpu.VMEM((slabs, seq_rows, LANES), F32))
    if two_stage:
        k1n, p = tabs["k1n"], tabs["p"]
        n2 = HY_N2
        body = functools.partial(_conv2_body, n=n, k1n=k1n, p=p, conv_u=conv_u)
        specs = common + [pl.BlockSpec((k1n, 2 * n2, cb), lambda j, i: (0, 0, order * per + j),
                                       pipeline_mode=once),
                          const((n2, 2 * p, n // n2)), const((2 * n2, 2 * n2)), const((2 * n2, 2 * n2)),
                          const((n2, n // n2, 2 * p))]
        args += [kf, tabs["first"], tabs["second"], tabs["second_inv"], tabs["last"]]
        scratch += [pltpu.VMEM((slabs, n2 * _pitch(2 * p), LANES), F32),
                    pltpu.VMEM((slabs, p * _pitch(2 * n2), LANES), F32)]
        name = "hy_conv2"
    else:
        p = tabs["p"]
        body = functools.partial(_conv1_body, n=n, p=p, conv_u=conv_u)
        specs = common + [pl.BlockSpec((2 * p, cb), lambda j, i: (0, order * per + j)),
                          const((2 * p, n)), const((n, 2 * p))]
        args += [kf, tabs["fwd"], tabs["inv"]]
        name = "hy_conv1"
    return pl.pallas_call(
        body,
        grid=(per, b),
        in_specs=specs,
        out_specs=pl.BlockSpec((1, n, cb), lambda j, i: (i, 0, j)),
        out_shape=jax.ShapeDtypeStruct((b, n, c), BF16),
        scratch_shapes=scratch,
        compiler_params=pltpu.CompilerParams(dimension_semantics=("parallel", "parallel"),
                                             vmem_limit_bytes=HY_VMEM_LIMIT),
        name=name,
    )(*args)


def _hyena_branch(big, conv_w, conv_b, bias, filt_params, two_stage):
    b, n, _ = big.shape
    kfilt = _hy_filter(n, *filt_params)
    if two_stage:
        tabs = _two_stage_tables(n, 2 * n)
        kf = _hy_spectrum2(kfilt, tabs)
    else:
        tabs = _one_stage_tables(n)
        kf = _hy_spectrum1(kfilt, tabs)
    cw = conv_w.astype(F32)
    cbias = conv_b.astype(F32).reshape(1, -1)
    z = _hy_long_conv(big, BLK_HY, True, big, BLK_HY + 1, cw, cbias, bias, kf, 0, tabs, two_stage)
    return _hy_long_conv(z, 0, False, big, BLK_HY + 2, cw, cbias, bias, kf, 1, tabs, two_stage)


def _rope_tables(n):
    rows = n // GRID_W
    row = jnp.repeat(jnp.arange(rows), GRID_W)
    col = jnp.tile(jnp.arange(GRID_W), rows)
    quarter = DA_HEAD_DIM // 4
    inv = ROPE_BASE ** (-jnp.arange(quarter, dtype=F32) / quarter)
    ang = jnp.stack([row, col], axis=-1).astype(F32)[..., None] * inv
    cos, sin = jnp.cos(ang), jnp.sin(ang)
    cos_h = jnp.concatenate([cos, cos], axis=-1).reshape(n, DA_HEAD_DIM)
    sin_h = jnp.concatenate([-sin, sin], axis=-1).reshape(n, DA_HEAD_DIM)
    return jnp.tile(cos_h, (1, 2)), jnp.tile(sin_h, (1, 2))


def _attn_body(*refs, nseg, lam_init):
    lp_ref, g_ref, q_ref = refs[:3]
    k_refs = refs[3:3 + nseg]
    v_refs = refs[3 + nseg:3 + 2 * nseg]
    o_ref = refs[3 + 2 * nseg]
    lp = lp_ref[...]
    lam = (jnp.exp(jnp.sum(lp[0:1] * lp[1:2], axis=1, keepdims=True))
           - jnp.exp(jnp.sum(lp[2:3] * lp[3:4], axis=1, keepdims=True)) + lam_init)
    scale = DA_HEAD_DIM ** -0.5 * math.log2(math.e)
    tq = q_ref.shape[1]
    sub = min(tq, ATT_SUB)
    lane = lax.broadcasted_iota(jnp.int32, (sub, 2 * DA_HEAD_DIM), 1)
    tiles = [pl.ds(r, sub) for r in range(0, tq, sub)]
    scores = []
    for rows in tiles:
        q = (q_ref[0, rows, :].astype(F32) * scale).astype(BF16)
        per_map = []
        for mp in range(2):
            qm = jnp.where((lane // DA_HEAD_DIM) == mp, q, jnp.zeros_like(q))
            per_map.append([lax.dot_general(qm, k_ref[0], (((1,), (1,)), ((), ())),
                                            preferred_element_type=F32) for k_ref in k_refs])
        scores.append(per_map)
    weights = []
    for per_map in scores:
        probs = []
        for s in per_map:
            mx = functools.reduce(jnp.maximum, [jnp.max(t, axis=1, keepdims=True) for t in s])
            e = [jnp.exp2(t - mx) for t in s]
            tot = functools.reduce(jnp.add, [jnp.sum(t, axis=1, keepdims=True) for t in e])
            probs.append((e, 1.0 / tot))
        (e0, r0), (e1, r1) = probs
        r1 = r1 * lam
        weights.append([(t0 * r0 - t1 * r1).astype(BF16) for t0, t1 in zip(e0, e1)])
    for rows, a in zip(tiles, weights):
        o = None
        for t, v_ref in zip(a, v_refs):
            part = jnp.dot(t, v_ref[0], preferred_element_type=F32)
            o = part if o is None else o + part
        o_ref[0, rows, :] = (_rms(o, g_ref[...]) * (1.0 - lam_init)).astype(o_ref.dtype)


def _diff_attention(q_arr, q_blk0, kv_segs, lam_p, norm_g, lam_init):
    b, lq, _ = q_arr.shape
    tq = min(lq, ATT_TQ)
    nseg = len(kv_segs)
    hw = 2 * DA_HEAD_DIM
    specs = [pl.BlockSpec((4, DA_HEAD_DIM), lambda i, h, m: (0, 0)),
             pl.BlockSpec((1, hw), lambda i, h, m: (0, 0)),
             pl.BlockSpec((1, tq, hw), lambda i, h, m: (i, m, q_blk0 + h))]
    args = [lam_p.astype(F32), norm_g.reshape(1, hw), q_arr]
    for k_arr, k_blk0, _, _ in kv_segs:
        specs.append(pl.BlockSpec((1, k_arr.shape[1], hw), functools.partial(
            lambda i, h, m, o: (i, 0, o + h), o=k_blk0)))
        args.append(k_arr)
    for _, _, v_arr, v_blk0 in kv_segs:
        specs.append(pl.BlockSpec((1, v_arr.shape[1], hw), functools.partial(
            lambda i, h, m, o: (i, 0, o + h), o=v_blk0)))
        args.append(v_arr)
    return pl.pallas_call(
        functools.partial(_attn_body, nseg=nseg, lam_init=lam_init),
        grid=(b, DA_HEADS, lq // tq),
        in_specs=specs,
        out_specs=pl.BlockSpec((1, tq, hw), lambda i, h, m: (i, m, h)),
        out_shape=jax.ShapeDtypeStruct((b, lq, DA_WIDTH), BF16),
        compiler_params=_cparams("parallel", "parallel", "arbitrary"),
        name="diff_attention",
    )(*args)


def _merge_body(s_ref, h_ref, a_ref, g0_ref, g1_ref, g2_ref, wb_ref, wo_ref, x_ref, m_ref, ng_ref, o_ref):
    mixed = None
    for br, gl, i in ((s_ref, g0_ref, 0), (h_ref, g1_ref, 1), (a_ref, g2_ref, 2)):
        t = jax.nn.sigmoid(gl[0].astype(F32)) * jnp.dot(br[0], wb_ref[i], preferred_element_type=F32)
        mixed = t if mixed is None else mixed + t
    out = jnp.dot(mixed.astype(BF16), wo_ref[...], preferred_element_type=F32)
    o_ref[0] = x_ref[0] + m_ref[0] * _rms(out, ng_ref[...])


def _merge_residual(ssd, hy, da, big, w_branch, w_out, x, mod_gate, norm_g):
    b, l, d = x.shape
    tm = min(l, 512)
    row = lambda col: pl.BlockSpec((1, tm, d), lambda i, m: (i, m, col))
    return pl.pallas_call(
        _merge_body,
        grid=(b, l // tm),
        in_specs=[row(0), row(0), row(0), row(BLK_GATE), row(BLK_GATE + 1), row(BLK_GATE + 2),
                  pl.BlockSpec((3, d, d), lambda i, m: (0, 0, 0)),
                  pl.BlockSpec((d, d), lambda i, m: (0, 0)),
                  row(0),
                  pl.BlockSpec((1, 1, d), lambda i, m: (i, 0, 0)),
                  pl.BlockSpec((1, d), lambda i, m: (0, 0))],
        out_specs=row(0),
        out_shape=jax.ShapeDtypeStruct((b, l, d), F32),
        compiler_params=_cparams("parallel", "parallel"),
        name="merge_residual",
    )(ssd, hy, da, big, big, big, w_branch, w_out, x, mod_gate.reshape(b, 1, d), norm_g.reshape(1, d))


def _ffn_down_body(gt_ref, p_ref, n_ref, up_ref, cw_ref, cb_ref, wd_ref, x_ref, m_ref, ng_ref, o_ref):
    m, nm = pl.program_id(1), pl.num_programs(1)
    gate = _conv_rows(gt_ref[0].astype(F32), p_ref[0].astype(F32), n_ref[0].astype(F32),
                      cw_ref[...], cb_ref[...], m, nm)
    act = (_silu(gate) * up_ref[0].astype(F32)).astype(BF16)
    out = jnp.dot(act, wd_ref[...], preferred_element_type=F32)
    o_ref[0] = x_ref[0] + m_ref[0] * _rms(out, ng_ref[...])


def _ffn_down_residual(up, conv_w, conv_b, w_down, x, mod_gate, norm_g):
    b, l, d = x.shape
    f = D_FF
    tm = min(l, 512)
    prev, nxt = _halo_specs(tm, l, f, 0)
    return pl.pallas_call(
        _ffn_down_body,
        grid=(b, l // tm),
        in_specs=[pl.BlockSpec((1, tm, f), lambda i, m: (i, m, 0)), prev, nxt,
                  pl.BlockSpec((1, tm, f), lambda i, m: (i, m, 1)),
                  pl.BlockSpec((FFN_CONV, f), lambda i, m: (0, 0)),
                  pl.BlockSpec((1, f), lambda i, m: (0, 0)),
                  pl.BlockSpec((f, d), lambda i, m: (0, 0)),
                  pl.BlockSpec((1, tm, d), lambda i, m: (i, m, 0)),
                  pl.BlockSpec((1, 1, d), lambda i, m: (i, 0, 0)),
                  pl.BlockSpec((1, d), lambda i, m: (0, 0))],
        out_specs=pl.BlockSpec((1, tm, d), lambda i, m: (i, m, 0)),
        out_shape=jax.ShapeDtypeStruct((b, l, d), F32),
        compiler_params=_cparams("parallel", "parallel"),
        name="ffn_down_residual",
    )(up, up, up, up, conv_w.astype(F32), conv_b.astype(F32).reshape(1, f), w_down, x,
      mod_gate.reshape(b, 1, d), norm_g.reshape(1, d))


def _project(x, g, shift, scale, w_big, w_xbc, w_dt, rope=None):
    big = _norm_mod_matmul(x, g, shift, scale, w_big, PROJ_TM, D_MODEL, BF16, "proj_big", rope)
    xbc = _norm_mod_matmul(x, g, shift, scale, w_xbc, PROJ_TM, 512, BF16, "proj_xbc")
    dt = _norm_mod_matmul(x, g, shift, scale, w_dt, PROJ_TM, LANES, F32, "proj_dt")
    return big, xbc, dt


def _lane_row(v):
    flat = v.astype(F32).reshape(1, 2 * SSD_HEADS)
    return jnp.pad(flat, ((0, 0), (0, LANES - 2 * SSD_HEADS)))


def kernel(x, c, ctx, c_ctx, w_ada, b_ada, norm_g, w_in, ssd_conv_w, ssd_conv_b, ssd_a_log, ssd_dt_bias, ssd_d, ssd_norm, hy_conv_w, hy_conv_b, hy_w1, hy_b1, hy_f1, hy_w2, hy_b2, hy_f2, hy_w3, hy_bias, da_lambda, da_norm, w_branch, w_out, ffn_w_up, ffn_conv_w, ffn_conv_b, ffn_w_down):
    b, seq, d = x.shape
    depth = w_ada.shape[0]
    cos_t, sin_t = _rope_tables(seq)
    mod = _ada_mod(c, c_ctx, w_ada, b_ada)
    x_l, x_c = x, ctx
    hshape = (b, SSD_GROUPS, SSD_STATE, SSD_GROUP_W)
    for i in range(depth):
        ctx_out = i < depth - 1
        ml = [mod[i, :b, k * d:(k + 1) * d] for k in range(6)]
        mc = [jnp.broadcast_to(mod[i, b:b + 1, k * d:(k + 1) * d], (b, d)) for k in range(6)]
        ng = norm_g[i]
        wi = w_in[i]
        w_big = jnp.concatenate([wi[:, :SSD_INNER], wi[:, OFF_HY:]], axis=1).astype(BF16)
        w_xbc = wi[:, SSD_INNER:SSD_INNER + SSD_CONV_CH].astype(BF16)
        w_dt = jnp.pad(wi[:, SSD_INNER + SSD_CONV_CH:IN_SSD], ((0, 0), (0, LANES - 2 * SSD_HEADS))).astype(BF16)
        big_c, xbc_c, dt_c = _project(x_c, ng[0], mc[0], mc[1], w_big, w_xbc, w_dt)
        big_l, xbc_l, dt_l = _project(x_l, ng[0], ml[0], ml[1], w_big, w_xbc, w_dt,
                                      (cos_t, sin_t, (BLK_Q, BLK_K)))

        conv_w = ssd_conv_w[i].astype(F32)
        dtb, alog = _lane_row(ssd_dt_bias[i]), _lane_row(ssd_a_log[i])
        dskip_x = jnp.repeat(ssd_d[i].astype(F32), SSD_HEAD_DIM).reshape(1, SSD_INNER)
        h0 = jnp.zeros(hshape, F32)
        ssd_c, hf, hb = _ssd_branch(_ssd_prep(xbc_c, conv_w, ssd_conv_b[i].astype(F32)), dt_c, big_c,
                                    dtb, alog, dskip_x, ssd_norm[i], h0, h0)
        ssd_l, _, _ = _ssd_branch(_ssd_prep(xbc_l, conv_w, ssd_conv_b[i].astype(F32)), dt_l, big_l,
                                  dtb, alog, dskip_x, ssd_norm[i], hf, hb)

        filt = (hy_w1[i], hy_b1[i], hy_f1[i], hy_w2[i], hy_b2[i], hy_f2[i], hy_w3[i])
        hy_l = _hyena_branch(big_l, hy_conv_w[i], hy_conv_b[i], hy_bias[i].astype(F32), filt, True)

        lam_init = 0.8 - 0.6 * math.exp(-0.3 * i)
        hpb = D_MODEL // (2 * DA_HEAD_DIM)
        segs = [(big_c, BLK_K * hpb, big_c, BLK_V * hpb), (big_l, BLK_K * hpb, big_l, BLK_V * hpb)]
        da_l = _diff_attention(big_l, BLK_Q * hpb, segs, da_lambda[i], da_norm[i], lam_init)

        wb = w_branch[i].astype(BF16)
        wo = w_out[i].astype(BF16)
        w_up = ffn_w_up[i].astype(BF16)
        w_dn = ffn_w_down[i].astype(BF16)
        x_l = _merge_residual(ssd_l, hy_l, da_l, big_l, wb, wo, x_l, ml[2], ng[1])
        up_l = _norm_mod_matmul(x_l, ng[2], ml[3], ml[4], w_up, FFN_TM, 1408, BF16, "ffn_up")
        x_l = _ffn_down_residual(up_l, ffn_conv_w[i], ffn_conv_b[i], w_dn, x_l, ml[5], ng[3])
        if ctx_out:
            hy_c = _hyena_branch(big_c, hy_conv_w[i], hy_conv_b[i], hy_bias[i].astype(F32), filt, False)
            da_c = _diff_attention(big_c, BLK_Q * hpb, [(big_c, BLK_K * hpb, big_c, BLK_V * hpb)],
                                   da_lambda[i], da_norm[i], lam_init)
            x_c = _merge_residual(ssd_c, hy_c, da_c, big_c, wb, wo, x_c, mc[2], ng[1])
            up_c = _norm_mod_matmul(x_c, ng[2], mc[3], mc[4], w_up, FFN_TM, 1408, BF16, "ffn_up")
            x_c = _ffn_down_residual(up_c, ffn_conv_w[i], ffn_conv_b[i], w_dn, x_c, mc[5], ng[3])
    return x_l
```
